```python
import math
import jax, jax.numpy as jnp
from jax import lax
import numpy as np

D_MODEL = 2048
BATCH = 4
SEQ = 2048
DEPTH = 1
DEC_BATCH = 128
DEC_SEQ = 4
PAST_LEN = 16384
PAGE_SIZE = 128

MIX_WIDTH = D_MODEL
GLA_WIDTH = MIX_WIDTH // 2
S5_WIDTH = MIX_WIDTH - GLA_WIDTH
GLA_HEADS = 4
GLA_DV = GLA_WIDTH // GLA_HEADS
GLA_DK = GLA_DV // 2
GLA_KW = GLA_HEADS * GLA_DK
GLA_RANK = 16
GLA_TAU = 16.0
GLA_CHUNK = 64
S5_CH = 16
S5_GROUPS = S5_WIDTH // S5_CH
S5_STATE = 64
S5_DT_MIN = 1e-3
S5_DT_MAX = 1e-1
D_FF = 5632
CONV_W = 3
NORM_EPS = 1e-6
IN_WIDTH = 2 * GLA_KW + 2 * GLA_WIDTH + GLA_RANK + S5_WIDTH

kernel_name = "hybrid_gla_s5_convffn_adaln_step"


def rmsnorm(x, gain):
    xf = x.astype(jnp.float32)
    y = xf * lax.rsqrt(jnp.mean(xf * xf, axis=-1, keepdims=True) + NORM_EPS)
    return (y * gain.astype(jnp.float32)).astype(x.dtype)


def modulate(h, shift, scale):
    return h * (1 + scale[:, None, :]) + shift[:, None, :]


def gla_chunked(q, k, v, log_a, s0):
    f32 = jnp.float32
    b, l = q.shape[0], q.shape[1]
    c = math.gcd(l, GLA_CHUNK)
    n = l // c

    def to_chunks(t):
        return jnp.moveaxis(t.astype(f32).reshape(b, n, c, t.shape[2], t.shape[3]), 1, 0).swapaxes(2, 3)

    qs = to_chunks(q) * (GLA_DK ** -0.5)
    ks, vs, gs = to_chunks(k), to_chunks(v), to_chunks(log_a)
    causal = jnp.tril(jnp.ones((c, c), dtype=bool))[:, :, None]

    def step(s, inp):
        qc, kc, vc, gc = inp
        cum = jnp.cumsum(gc, axis=2)
        rel = cum[:, :, :, None, :] - cum[:, :, None, :, :]
        decay = jnp.exp(jnp.where(causal, rel, -jnp.inf))
        att = jnp.einsum('bhtd,bhsd,bhtsd->bhts', qc, kc, decay)
        o = (jnp.einsum('bhts,bhsv->bhtv', att, vc)
             + jnp.einsum('bhtd,bhdv->bhtv', qc * jnp.exp(cum), s))
        last = cum[:, :, -1:, :]
        s_new = (jnp.exp(last[:, :, 0, :])[..., None] * s
                 + jnp.einsum('bhsd,bhsv->bhdv', kc * jnp.exp(last - cum), vc))
        return s_new, o

    s_fin, os_ = lax.scan(step, s0.astype(f32), (qs, ks, vs, gs))
    o = jnp.moveaxis(os_.swapaxes(2, 3), 0, 1).reshape(b, l, GLA_HEADS, GLA_DV)
    return o, s_fin


def s5_scan(u, lam_re, lam_im, log_step, b_re, b_im, c_re, c_im, d_skip, h0_re, h0_im):
    f32 = jnp.float32
    lam = lax.complex(lam_re.astype(f32), lam_im.astype(f32))
    dt = jnp.exp(log_step.astype(f32))[:, None]
    lam_bar = jnp.exp(lam * dt)
    bmat = lax.complex(b_re.astype(f32), b_im.astype(f32))
    b_bar = ((lam_bar - 1) / lam)[..., None] * bmat
    cmat = lax.complex(c_re.astype(f32), c_im.astype(f32))
    uf = u.astype(f32)
    bu = jnp.einsum('gpk,blgk->blgp', b_bar, uf.astype(jnp.complex64))
    h0 = lax.complex(h0_re.astype(f32), h0_im.astype(f32))
    bu = bu.at[:, 0].add(lam_bar * h0)
    a = jnp.broadcast_to(lam_bar, bu.shape)

    def combine(e1, e2):
        a1, b1 = e1
        a2, b2 = e2
        return a1 * a2, a2 * b1 + b2

    _, h = lax.associative_scan(combine, (a, bu), axis=1)
    y = jnp.real(jnp.einsum('gkp,blgp->blgk', cmat, h)) + d_skip.astype(f32) * uf
    h_last = h[:, -1]
    return y, jnp.real(h_last), jnp.imag(h_last)


def layer(x, c, gla_s, s5_re, s5_im, conv_s, w):
    bsz, l = x.shape[0], x.shape[1]
    mod = jnp.einsum('bd,de->be', jax.nn.silu(c), w['w_ada']) + w['b_ada']
    sh1, sc1, g1, sh2, sc2, g2 = jnp.split(mod, 6, axis=-1)

    h = modulate(rmsnorm(x, w['norm1']), sh1, sc1)
    proj = jnp.einsum('bld,de->ble', h, w['w_in'])
    cuts = [GLA_KW, 2 * GLA_KW, 2 * GLA_KW + GLA_WIDTH, 2 * GLA_KW + 2 * GLA_WIDTH,
            2 * GLA_KW + 2 * GLA_WIDTH + GLA_RANK]
    q, k, v, g, a_lr, u = jnp.split(proj, cuts, axis=-1)
    q = q.reshape(bsz, l, GLA_HEADS, GLA_DK)
    k = k.reshape(bsz, l, GLA_HEADS, GLA_DK)
    v = v.reshape(bsz, l, GLA_HEADS, GLA_DV)
    log_a = jax.nn.log_sigmoid((a_lr @ w['w_a2'] + w['b_a2']).astype(jnp.float32)) / GLA_TAU
    log_a = log_a.reshape(bsz, l, GLA_HEADS, GLA_DK)
    o, gla_new = gla_chunked(q, k, v, log_a, gla_s)
    o = rmsnorm(o.astype(x.dtype), w['gla_norm']).reshape(bsz, l, GLA_WIDTH) * jax.nn.silu(g)

    y5, re_new, im_new = s5_scan(u.reshape(bsz, l, S5_GROUPS, S5_CH), w['s5_lam_re'], w['s5_lam_im'],
                                 w['s5_log_step'], w['s5_b_re'], w['s5_b_im'], w['s5_c_re'],
                                 w['s5_c_im'], w['s5_d'], s5_re, s5_im)
    y5 = jax.nn.gelu(y5.reshape(bsz, l, S5_WIDTH).astype(x.dtype))
    y5 = y5 * jax.nn.sigmoid(y5 @ w['w_glu'] + w['b_glu'])

    mix = jnp.concatenate([o, y5], axis=-1) @ w['w_out']
    x = x + g1[:, None, :] * mix

    h = modulate(rmsnorm(x, w['norm2']), sh2, sc2)
    up = jnp.einsum('bld,df->blf', h, w['w_up'])
    xpad = jnp.concatenate([conv_s.astype(up.dtype), up], axis=1)
    conv = lax.conv_general_dilated(xpad, w['conv_w'][:, None, :].astype(up.dtype), (1,), 'VALID',
                                    dimension_numbers=('NWC', 'WIO', 'NWC'),
                                    feature_group_count=2 * D_FF) + w['conv_b']
    a_, gt = jnp.split(conv, 2, axis=-1)
    ff = (jax.nn.gelu(a_) * gt) @ w['w_down']
    x = x + g2[:, None, :] * ff
    return x, gla_new, re_new, im_new, xpad[:, -(CONV_W - 1):]


def setup_inputs(seed: int = 0) -> dict:
    key = jax.random.key(seed)
    ks = jax.random.split(key, 40)
    f32 = jnp.float32
    nrm = lambda i, shape, s: jax.random.normal(ks[i], shape, f32) * s
    n_idx = jnp.arange(S5_STATE, dtype=f32)
    lam_re = -0.5 + nrm(10, (DEPTH, S5_GROUPS, S5_STATE), 1e-3)
    lam_im = jnp.broadcast_to(math.pi * n_idx, (DEPTH, S5_GROUPS, S5_STATE)).astype(f32)
    log_step = jax.random.uniform(ks[11], (DEPTH, S5_GROUPS), f32,
                                  math.log(S5_DT_MIN), math.log(S5_DT_MAX))
    return {
        "x_prompt": nrm(0, (BATCH, SEQ, D_MODEL), 1.0),
        "x_sample": nrm(1, (DEC_BATCH, DEC_SEQ, D_MODEL), 1.0),
        "c_prompt": nrm(2, (BATCH, D_MODEL), 1.0),
        "c_sample": nrm(3, (DEC_BATCH, D_MODEL), 1.0),
        "state_gla": nrm(4, (DEPTH, DEC_BATCH, GLA_HEADS, GLA_DK, GLA_DV), 0.5),
        "state_s5_re": nrm(5, (DEPTH, DEC_BATCH, S5_GROUPS, S5_STATE), 0.1),
        "state_s5_im": nrm(6, (DEPTH, DEC_BATCH, S5_GROUPS, S5_STATE), 0.1),
        "state_conv": nrm(7, (DEPTH, DEC_BATCH, CONV_W - 1, 2 * D_FF), 1.0),
        "w_ada": nrm(8, (DEPTH, D_MODEL, 6 * D_MODEL), 0.5 * D_MODEL ** -0.5),
        "b_ada": nrm(9, (DEPTH, 6 * D_MODEL), 0.02),
        "norm1": 1.0 + nrm(12, (DEPTH, D_MODEL), 0.01),
        "w_in": nrm(13, (DEPTH, D_MODEL, IN_WIDTH), D_MODEL ** -0.5),
        "w_a2": nrm(14, (DEPTH, GLA_RANK, GLA_KW), GLA_RANK ** -0.5),
        "b_a2": nrm(15, (DEPTH, GLA_KW), 0.1),
        "gla_norm": 1.0 + nrm(16, (DEPTH, GLA_DV), 0.01),
        "s5_lam_re": lam_re,
        "s5_lam_im": lam_im,
        "s5_log_step": log_step,
        "s5_b_re": nrm(17, (DEPTH, S5_GROUPS, S5_STATE, S5_CH), (2 * S5_CH) ** -0.5),
        "s5_b_im": nrm(18, (DEPTH, S5_GROUPS, S5_STATE, S5_CH), (2 * S5_CH) ** -0.5),
        "s5_c_re": nrm(19, (DEPTH, S5_GROUPS, S5_CH, S5_STATE), S5_STATE ** -0.5),
        "s5_c_im": nrm(20, (DEPTH, S5_GROUPS, S5_CH, S5_STATE), S5_STATE ** -0.5),
        "s5_d": nrm(21, (DEPTH, S5_GROUPS, S5_CH), 1.0),
        "w_glu": nrm(22, (DEPTH, S5_WIDTH, S5_WIDTH), S5_WIDTH ** -0.5),
        "b_glu": nrm(23, (DEPTH, S5_WIDTH), 0.02),
        "w_out": nrm(24, (DEPTH, MIX_WIDTH, D_MODEL), MIX_WIDTH ** -0.5),
        "norm2": 1.0 + nrm(25, (DEPTH, D_MODEL), 0.01),
        "w_up": nrm(26, (DEPTH, D_MODEL, 2 * D_FF), D_MODEL ** -0.5),
        "conv_w": nrm(27, (DEPTH, CONV_W, 2 * D_FF), CONV_W ** -0.5),
        "conv_b": nrm(28, (DEPTH, 2 * D_FF), 0.02),
        "w_down": nrm(29, (DEPTH, D_FF, D_MODEL), D_FF ** -0.5),
        "final_norm": 1.0 + nrm(30, (D_MODEL,), 0.01),
    }


def reference(x_prompt, x_sample, c_prompt, c_sample, state_gla, state_s5_re, state_s5_im, state_conv,
              w_ada, b_ada, norm1, w_in, w_a2, b_a2, gla_norm, s5_lam_re, s5_lam_im, s5_log_step,
              s5_b_re, s5_b_im, s5_c_re, s5_c_im, s5_d, w_glu, b_glu, w_out, norm2, w_up, conv_w,
              conv_b, w_down, final_norm):
    f32 = jnp.float32
    bp = x_prompt.shape[0]
    yp, ys = x_prompt, x_sample
    gla_p, re_p, im_p, conv_p = [], [], [], []
    gla_s, re_s, im_s, conv_s = [], [], [], []
    for i in range(DEPTH):
        w = {
            'w_ada': w_ada[i], 'b_ada': b_ada[i], 'norm1': norm1[i], 'w_in': w_in[i],
            'w_a2': w_a2[i], 'b_a2': b_a2[i], 'gla_norm': gla_norm[i],
            's5_lam_re': s5_lam_re[i], 's5_lam_im': s5_lam_im[i], 's5_log_step': s5_log_step[i],
            's5_b_re': s5_b_re[i], 's5_b_im': s5_b_im[i], 's5_c_re': s5_c_re[i], 's5_c_im': s5_c_im[i],
            's5_d': s5_d[i], 'w_glu': w_glu[i], 'b_glu': b_glu[i], 'w_out': w_out[i],
            'norm2': norm2[i], 'w_up': w_up[i], 'conv_w': conv_w[i], 'conv_b': conv_b[i],
            'w_down': w_down[i],
        }
        yp, g_new, r_new, m_new, c_new = layer(
            yp, c_prompt,
            jnp.zeros((bp, GLA_HEADS, GLA_DK, GLA_DV), f32),
            jnp.zeros((bp, S5_GROUPS, S5_STATE), f32),
            jnp.zeros((bp, S5_GROUPS, S5_STATE), f32),
            jnp.zeros((bp, CONV_W - 1, 2 * D_FF), yp.dtype), w)
        gla_p.append(g_new); re_p.append(r_new); im_p.append(m_new); conv_p.append(c_new)
        ys, g_new, r_new, m_new, c_new = layer(
            ys, c_sample, state_gla[i], state_s5_re[i], state_s5_im[i], state_conv[i], w)
        gla_s.append(g_new); re_s.append(r_new); im_s.append(m_new); conv_s.append(c_new)
    y_prompt = rmsnorm(yp, final_norm)
    y_sample = rmsnorm(ys, final_norm)
    return (y_prompt, y_sample,
            jnp.stack(gla_p), jnp.stack(re_p), jnp.stack(im_p), jnp.stack(conv_p),
            jnp.stack(gla_s), jnp.stack(re_s), jnp.stack(im_s), jnp.stack(conv_s))
```

```python
import functools
import math

import jax
import jax.numpy as jnp
from jax import lax
from jax.experimental import pallas as pl
from jax.experimental.pallas import tpu as pltpu

F32 = jnp.float32
BF16 = jnp.bfloat16

NORM_EPS = 1e-6
GLA_TAU = 16.0
GLA_CHUNK = 64
GLA_SUB = 16
LANES = 128
GLA_PAD = 128
S5_CHUNK = 16
S5_CHUNK_SAMPLE = 8
VMEM_LIMIT_BYTES = 56 * 1024 * 1024
TM_PROMPT = 512
TM_MIX = 256


def _cparams(*sem):
    return pltpu.CompilerParams(dimension_semantics=sem, vmem_limit_bytes=VMEM_LIMIT_BYTES)


def _dot(a, b):
    return jnp.dot(a.astype(BF16), b.astype(BF16), preferred_element_type=F32)


def _dot_nt(a, b):
    return lax.dot_general(a.astype(BF16), b.astype(BF16), (((1,), (1,)), ((), ())),
                           preferred_element_type=F32)


def _dot_f32(a, b):
    return jnp.dot(a, b, preferred_element_type=F32, precision=lax.Precision.HIGHEST)


def _rms(x):
    return x * lax.rsqrt(jnp.mean(x * x, axis=-1, keepdims=True) + NORM_EPS)


def _gelu(x):
    return 0.5 * x * (1.0 + jnp.tanh(math.sqrt(2.0 / math.pi) * (x + 0.044715 * (x * x * x))))


def _sigmoid(x):
    return 1.0 / (1.0 + jnp.exp(-x))


def _ada_kernel(c_ref, w_ref, b_ref, o_ref):
    c = c_ref[...]
    o_ref[...] = _dot(c * _sigmoid(c), w_ref[...]) + b_ref[...]


def _ada(c_all, w_ada, b_ada, tn=1024):
    m, d = c_all.shape
    n = w_ada.shape[1]
    return pl.pallas_call(
        _ada_kernel,
        grid=(n // tn,),
        in_specs=[pl.BlockSpec((m, d), lambda j: (0, 0)),
                  pl.BlockSpec((d, tn), lambda j: (0, j)),
                  pl.BlockSpec((1, tn), lambda j: (0, j))],
        out_specs=pl.BlockSpec((m, tn), lambda j: (0, j)),
        out_shape=jax.ShapeDtypeStruct((m, n), F32),
        compiler_params=_cparams("arbitrary"),
        name="ada",
    )(c_all, w_ada, b_ada)


def _inproj_kernel(x_ref, sh_ref, sc_ref, n1_ref, wm_ref, wu_ref, wa_ref, wa2_ref, ba2_ref,
                   qkvg_ref, u_ref, la_ref, h_scr, *, n_slabs, n_main):
    j = pl.program_id(1)
    rows = x_ref.shape[0] // n_slabs

    @pl.when(j == 0)
    def _():
        for l in range(n_slabs):
            sl = slice(l * rows, (l + 1) * rows)
            h = _rms(x_ref[sl, :]) * n1_ref[...] * (1.0 + sc_ref[...]) + sh_ref[...]
            h_scr[sl, :] = h.astype(BF16)
        a_lr = jnp.dot(h_scr[...], wa_ref[...], preferred_element_type=F32)
        z = _dot(a_lr, wa2_ref[...]) + ba2_ref[...]
        la_ref[...] = (jnp.minimum(z, 0.0) - jnp.log(1.0 + jnp.exp(-jnp.abs(z)))) / GLA_TAU

    @pl.when(j < n_main)
    def _():
        qkvg_ref[...] = jnp.dot(h_scr[...], wm_ref[...], preferred_element_type=F32)

    @pl.when(j >= n_main)
    def _():
        u_ref[...] = jnp.dot(h_scr[...], wu_ref[...], preferred_element_type=F32)


def _inproj(x, mod, mod_row, norm1, w_in_bf, w_u_bf, w_a_bf, w_a2, b_a2, *, tm, n_slabs, main_w, tn=512):
    r, d = x.shape
    n_main = main_w // tn
    n_u = w_u_bf.shape[1] // tn
    kw = w_a2.shape[1]
    rank = w_a_bf.shape[1]
    if mod.ndim == 3:
        mspec = lambda k: pl.BlockSpec((None, 1, d), lambda i, j: (mod_row(i), 0, k))
    else:
        mspec = lambda k: pl.BlockSpec((mod.shape[0], d), lambda i, j: (0, k))
    kern = functools.partial(_inproj_kernel, n_slabs=n_slabs, n_main=n_main)
    return pl.pallas_call(
        kern,
        grid=(r // tm, n_main + n_u),
        in_specs=[pl.BlockSpec((tm, d), lambda i, j: (i, 0)),
                  mspec(0), mspec(1),
                  pl.BlockSpec((1, d), lambda i, j: (0, 0)),
                  pl.BlockSpec((d, tn), lambda i, j: (0, jnp.minimum(j, n_main - 1))),
                  pl.BlockSpec((d, tn), lambda i, j: (0, jnp.maximum(j - n_main, 0))),
                  pl.BlockSpec((d, rank), lambda i, j: (0, 0)),
                  pl.BlockSpec((rank, kw), lambda i, j: (0, 0)),
                  pl.BlockSpec((1, kw), lambda i, j: (0, 0))],
        out_specs=[pl.BlockSpec((tm, tn), lambda i, j: (i, jnp.minimum(j, n_main - 1))),
                   pl.BlockSpec((tm, tn), lambda i, j: (i, jnp.maximum(j - n_main, 0))),
                   pl.BlockSpec((tm, kw), lambda i, j: (i, 0))],
        out_shape=[jax.ShapeDtypeStruct((r, main_w), F32),
                   jax.ShapeDtypeStruct((r, w_u_bf.shape[1]), F32),
                   jax.ShapeDtypeStruct((r, kw), F32)],
        scratch_shapes=[pltpu.VMEM((tm, d), BF16)],
        compiler_params=_cparams("arbitrary", "arbitrary"),
        name="inproj",
    )(x, mod, mod, norm1, w_in_bf, w_u_bf, w_a_bf, w_a2, b_a2)


def _gla_chunk(q, k, v, ga, s, *, sub):
    c, dk = q.shape
    tri = (lax.broadcasted_iota(jnp.int32, (c, c), 0) >= lax.broadcasted_iota(jnp.int32, (c, c), 1))
    tri = tri.astype(F32).astype(BF16)
    ga_hi = ga.astype(BF16)
    ga_lo = (ga - ga_hi.astype(F32)).astype(BF16)
    cum = (jnp.dot(tri, ga_hi, preferred_element_type=F32)
           + jnp.dot(tri, ga_lo, preferred_element_type=F32))
    o_inter = _dot(q * jnp.exp(cum), s)

    row_sub = lax.broadcasted_iota(jnp.int32, (sub, dk), 0)
    lane_ss = lax.broadcasted_iota(jnp.int32, (sub, sub), 1)
    blocks = []
    for blk in range(c // sub):
        r0 = blk * sub
        q_b, k_b, c_b, v_b = q[r0:r0 + sub], k[r0:r0 + sub], cum[r0:r0 + sub], v[r0:r0 + sub]
        att = jnp.zeros((sub, sub), F32)
        for si in range(sub):
            keep = row_sub >= si
            dec = jnp.where(keep, jnp.exp(jnp.where(keep, c_b - c_b[si:si + 1], 0.0)), 0.0)
            col = jnp.sum(q_b * k_b[si:si + 1] * dec, axis=-1, keepdims=True)
            att = jnp.where(lane_ss == si, col, att)
        o_b = _dot(att, v_b)
        if blk > 0:
            ref = cum[r0 - 1:r0]
            q_t = q_b * jnp.exp(c_b - ref)
            k_t = k[:r0] * jnp.exp(ref - cum[:r0])
            o_b = o_b + _dot(_dot_nt(q_t, k_t), v[:r0])
        blocks.append(o_b)
    o = o_inter + jnp.concatenate(blocks, axis=0)

    last = cum[c - 1:c]
    k_out = k * jnp.exp(last - cum)
    tail_rows = GLA_PAD - c
    tail = jnp.where(lax.broadcasted_iota(jnp.int32, (tail_rows, dk), 0) == 0, jnp.exp(last), 0.0)
    kt = jnp.concatenate([k_out, tail], axis=0).T
    s_new = s * kt[:, c:c + 1] + _dot(kt[:, :c], v)
    return o, s_new


def _gla_finish(o, g, gn):
    return _rms(o) * gn * (g * _sigmoid(g))


def _gla_prompt_kernel(q_ref, k_ref, v_ref, g_ref, la_ref, gn_ref, o_ref, s_ref, s_scr, *, chunk, sub, scale):
    s_scr[...] = jnp.zeros_like(s_scr)

    def body(n, carry):
        r = pl.ds(pl.multiple_of(n * chunk, chunk), chunk)
        o, s_new = _gla_chunk(q_ref[r, :] * scale, k_ref[r, :], v_ref[r, :], la_ref[r, :], s_scr[...], sub=sub)
        s_scr[...] = s_new
        o_ref[r, :] = _gla_finish(o, g_ref[r, :], gn_ref[...])
        return carry

    lax.fori_loop(0, q_ref.shape[0] // chunk, body, 0)
    s_ref[...] = s_scr[...]


def _gla_prompt(qkvg, la, gla_norm, *, b, l, h, dk, dv):
    kw, gw = h * dk, h * dv
    chunk = math.gcd(l, GLA_CHUNK)
    sub = math.gcd(chunk, GLA_SUB)
    kern = functools.partial(_gla_prompt_kernel, chunk=chunk, sub=sub, scale=dk ** -0.5)
    return pl.pallas_call(
        kern,
        grid=(b, h),
        in_specs=[pl.BlockSpec((l, dk), lambda i, j: (i, j)),
                  pl.BlockSpec((l, dk), lambda i, j: (i, kw // dk + j)),
                  pl.BlockSpec((l, dv), lambda i, j: (i, 2 * kw // dv + j)),
                  pl.BlockSpec((l, dv), lambda i, j: (i, (2 * kw + gw) // dv + j)),
                  pl.BlockSpec((l, dk), lambda i, j: (i, j)),
                  pl.BlockSpec((1, dv), lambda i, j: (0, 0))],
        out_specs=[pl.BlockSpec((l, dv), lambda i, j: (i, j)),
                   pl.BlockSpec((None, None, dk, dv), lambda i, j: (i, j, 0, 0))],
        out_shape=[jax.ShapeDtypeStruct((b * l, gw), F32),
                   jax.ShapeDtypeStruct((b, h, dk, dv), F32)],
        scratch_shapes=[pltpu.VMEM((dk, dv), F32)],
        compiler_params=_cparams("arbitrary", "arbitrary"),
        name="gla_prompt",
    )(qkvg, qkvg, qkvg, qkvg, la, gla_norm)


def _gla_sample_kernel(qkvg_ref, la_ref, gn_ref, s0_ref, o_ref, s_ref, *, nb, seq, h, dk, dv, pad, scale):
    bt = s0_ref.shape[0]
    kw, gw = h * dk, h * dv
    in_w = 2 * kw + 2 * gw
    i = pl.program_id(0)

    def rows_of(b, width, col):
        per_row = width // LANES
        return pl.ds(b * per_row + col // LANES, seq, stride=nb * per_row)

    def gather(ref, b, width, col, n_col):
        parts = [ref[rows_of(b, width, col + c), :] for c in range(0, n_col, LANES)]
        x = parts[0] if len(parts) == 1 else jnp.concatenate(parts, axis=1)
        return jnp.concatenate([x, jnp.zeros((pad - seq, n_col), F32)], axis=0)

    def body(bb, carry):
        b = i * bt + bb
        for hh in range(h):
            q = gather(qkvg_ref, b, in_w, hh * dk, dk) * scale
            k = gather(qkvg_ref, b, in_w, kw + hh * dk, dk)
            v = gather(qkvg_ref, b, in_w, 2 * kw + hh * dv, dv)
            ga = gather(la_ref, b, kw, hh * dk, dk)
            o, s_new = _gla_chunk(q, k, v, ga, s0_ref[bb, hh], sub=pad)
            s_ref[bb, hh] = s_new
            g = gather(qkvg_ref, b, in_w, 2 * kw + gw + hh * dv, dv)
            res = _gla_finish(o, g, gn_ref[...])
            for c in range(0, dv, LANES):
                o_ref[rows_of(b, gw, hh * dv + c), :] = res[:seq, c:c + LANES]
        return carry

    lax.fori_loop(0, bt, body, 0)


def _gla_sample(qkvg, la, gla_norm, s0, *, nb, seq, h, dk, dv, bt=8):
    gw = h * dv
    pad = 8
    assert dk % LANES == 0 and dv % LANES == 0
    qkvg, la = qkvg.reshape(-1, LANES), la.reshape(-1, LANES)
    o_rows = nb * seq * gw // LANES
    kern = functools.partial(_gla_sample_kernel, nb=nb, seq=seq, h=h, dk=dk, dv=dv, pad=pad, scale=dk ** -0.5)
    o, s_new = pl.pallas_call(
        kern,
        grid=(nb // bt,),
        in_specs=[pl.BlockSpec(qkvg.shape, lambda i: (0, 0)),
                  pl.BlockSpec(la.shape, lambda i: (0, 0)),
                  pl.BlockSpec((1, dv), lambda i: (0, 0)),
                  pl.BlockSpec((bt, h, dk, dv), lambda i: (i, 0, 0, 0))],
        out_specs=[pl.BlockSpec((o_rows, LANES), lambda i: (0, 0)),
                   pl.BlockSpec((bt, h, dk, dv), lambda i: (i, 0, 0, 0))],
        out_shape=[jax.ShapeDtypeStruct((o_rows, LANES), F32),
                   jax.ShapeDtypeStruct((nb, h, dk, dv), F32)],
        compiler_params=_cparams("arbitrary"),
        name="gla_sample",
    )(qkvg, la, gla_norm, s0)
    return o.reshape(nb * seq, gw), s_new


def _s5_prep_kernel(lamc_ref, lamr_ref, ls_ref, bt_re_ref, bt_im_ref, btile_re_ref, btile_im_ref,
                    ctile_re_ref, ctile_im_ref, d_ref, m_ref, wst_ref, v_ref, lc_ref, *, ch, c_mat, c_real):
    w = ch * c_mat
    dt = jnp.exp(ls_ref[...])

    def discretize(lam_re, lam_im):
        a, th = lam_re * dt, lam_im * dt
        ea = jnp.exp(a)
        lb_re, lb_im = ea * jnp.cos(th), ea * jnp.sin(th)
        den = lam_re * lam_re + lam_im * lam_im
        x, y = lb_re - 1.0, lb_im
        return a, th, (x * lam_re + y * lam_im) / den, (y * lam_re - x * lam_im) / den

    a, th, cf_re, cf_im = discretize(lamc_ref[:, 0:1], lamc_ref[:, 1:2])
    p = a.shape[0]
    tau = (lax.broadcasted_iota(jnp.int32, (p, w), 1) // ch).astype(F32)

    def powers(t):
        e = jnp.exp(a * t)
        return e * jnp.cos(th * t), e * jnp.sin(th * t)

    ct_re, ct_im = ctile_re_ref[...], ctile_im_ref[...]
    pw_re, pw_im = powers(tau)
    cl_re = ct_re * pw_re - ct_im * pw_im
    cl_im = ct_re * pw_im + ct_im * pw_re
    pv_re, pv_im = powers(tau + 1.0)
    v_ref[0:p, :] = ct_re * pv_re - ct_im * pv_im
    v_ref[p:2 * p, :] = -(ct_re * pv_im + ct_im * pv_re)

    bb_re = cf_re * btile_re_ref[...] - cf_im * btile_im_ref[...]
    bb_im = cf_re * btile_im_ref[...] + cf_im * btile_re_ref[...]
    ps_re, ps_im = powers(jnp.maximum(float(c_real - 1) - tau, 0.0))
    wst_ref[0:p, :] = bb_re * ps_re - bb_im * ps_im
    wst_ref[p:2 * p, :] = bb_re * ps_im + bb_im * ps_re

    _, _, rf_re, rf_im = discretize(lamr_ref[0:1, :], lamr_ref[1:2, :])
    bbt_re = rf_re * bt_re_ref[...] - rf_im * bt_im_ref[...]
    bbt_im = rf_re * bt_im_ref[...] + rf_im * bt_re_ref[...]
    kcat = _dot_f32(bbt_re, cl_re) - _dot_f32(bbt_im, cl_im)
    rr = lax.broadcasted_iota(jnp.int32, (ch, w), 0)
    cc = lax.broadcasted_iota(jnp.int32, (ch, w), 1)
    kcat = kcat + jnp.where(rr == cc, d_ref[...], 0.0)
    for i in range(c_mat):
        shifted = kcat if i == 0 else pltpu.roll(kcat, ch * i, axis=1)
        m_ref[i * ch:(i + 1) * ch, :] = jnp.where(cc >= ch * i, shifted, 0.0)

    ar, tr = lamr_ref[0:1, :] * dt, lamr_ref[1:2, :] * dt
    ec = jnp.exp(ar * float(c_real))
    lc_ref[0:1, :] = ec * jnp.cos(tr * float(c_real))
    lc_ref[1:2, :] = ec * jnp.sin(tr * float(c_real))


def _s5_prep(lam_re, lam_im, log_step, b_re, b_im, c_re, c_im, d_skip, *, c_mat, c_real):
    g, p, ch = b_re.shape
    w = ch * c_mat
    lam_col = jnp.stack([lam_re, lam_im], axis=-1)
    lam_row = jnp.stack([lam_re, lam_im], axis=1)
    ls = log_step.reshape(g, 1, 1)
    tile = lambda t: jnp.tile(t, (1, 1, c_mat))
    bt = lambda t: jnp.swapaxes(t, 1, 2)
    d_row = jnp.pad(d_skip.reshape(g, 1, ch), ((0, 0), (0, 0), (0, w - ch)))
    spec3 = lambda s: pl.BlockSpec((None,) + s, lambda i: (i, 0, 0))
    kern = functools.partial(_s5_prep_kernel, ch=ch, c_mat=c_mat, c_real=c_real)
    return pl.pallas_call(
        kern,
        grid=(g,),
        in_specs=[spec3((p, 2)), spec3((2, p)), spec3((1, 1)), spec3((ch, p)), spec3((ch, p)),
                  spec3((p, w)), spec3((p, w)), spec3((p, w)), spec3((p, w)), spec3((1, w))],
        out_specs=[spec3((w, w)), spec3((2 * p, w)), spec3((2 * p, w)), spec3((2, p))],
        out_shape=[jax.ShapeDtypeStruct((g, w, w), F32), jax.ShapeDtypeStruct((g, 2 * p, w), F32),
                   jax.ShapeDtypeStruct((g, 2 * p, w), F32), jax.ShapeDtypeStruct((g, 2, p), F32)],
        compiler_params=_cparams("arbitrary"),
        name="s5_prep",
    )(lam_col, lam_row, ls, bt(b_re), bt(b_im), tile(b_re), tile(b_im),
      tile(jnp.swapaxes(c_re, 1, 2)), tile(jnp.swapaxes(c_im, 1, 2)), d_row)


def _cmul_rows(z, lr, li):
    p = lr.shape[1]
    coef_a = jnp.concatenate([lr, lr], axis=1)
    coef_b = jnp.concatenate([-li, li], axis=1)
    return z * coef_a + pltpu.roll(z, p, axis=1) * coef_b


def _s5_prompt_kernel(x_ref, m_ref, wst_ref, v_ref, lc_ref, y_ref, hf_ref, *, n_seq, n_chunk):
    x = x_ref[...]
    z = _dot_nt(x, wst_ref[...])
    r = z.shape[0]
    pos = lax.broadcasted_iota(jnp.int32, z.shape, 0) % n_chunk
    lr, li = lc_ref[0:1, :], lc_ref[1:2, :]
    d = 1
    while d < n_chunk:
        zs = jnp.where(pos >= d, pltpu.roll(z, d, axis=0), 0.0)
        z = z + _cmul_rows(zs, lr, li)
        lr, li = lr * lr - li * li, 2.0 * lr * li
        d *= 2
    h_in = jnp.where(pos >= 1, pltpu.roll(z, 1, axis=0), 0.0)
    y_ref[...] = _dot(x, m_ref[...]) + _dot(h_in, v_ref[...])
    hf_ref[...] = jnp.concatenate([z[(s + 1) * n_chunk - 1:(s + 1) * n_chunk] for s in range(n_seq)], axis=0)


def _s5_prompt(x, m, wst, v, lc, *, n_seq, n_chunk):
    g, r, w = x.shape
    p2 = wst.shape[1]
    spec3 = lambda s: pl.BlockSpec((None,) + s, lambda i: (i, 0, 0))
    kern = functools.partial(_s5_prompt_kernel, n_seq=n_seq, n_chunk=n_chunk)
    return pl.pallas_call(
        kern,
        grid=(g,),
        in_specs=[spec3((r, w)), spec3((w, w)), spec3((p2, w)), spec3((p2, w)), spec3((2, p2 // 2))],
        out_specs=[spec3((r, w)), spec3((n_seq, p2))],
        out_shape=[jax.ShapeDtypeStruct((g, r, w), F32), jax.ShapeDtypeStruct((g, n_seq, p2), F32)],
        compiler_params=_cparams("arbitrary"),
        name="s5_prompt",
    )(x, m, wst, v, lc)


def _s5_sample_kernel(x_ref, h0_ref, m_ref, wst_ref, v_ref, lc_ref, y_ref, hf_ref):
    x = x_ref[...]
    h0 = h0_ref[...]
    y_ref[...] = _dot(x, m_ref[...]) + _dot(h0, v_ref[...])
    hf_ref[...] = _cmul_rows(h0, lc_ref[0:1, :], lc_ref[1:2, :]) + _dot_nt(x, wst_ref[...])


def _s5_sample(x, h0, m, wst, v, lc):
    g, r, w = x.shape
    p2 = wst.shape[1]
    spec3 = lambda s: pl.BlockSpec((None,) + s, lambda i: (i, 0, 0))
    return pl.pallas_call(
        _s5_sample_kernel,
        grid=(g,),
        in_specs=[spec3((r, w)), spec3((r, p2)), spec3((w, w)), spec3((p2, w)), spec3((p2, w)),
                  spec3((2, p2 // 2))],
        out_specs=[spec3((r, w)), spec3((r, p2))],
        out_shape=[jax.ShapeDtypeStruct((g, r, w), F32), jax.ShapeDtypeStruct((g, r, p2), F32)],
        compiler_params=_cparams("arbitrary"),
        name="s5_sample",
    )(x, h0, m, wst, v, lc)


def _mix_kernel(o_ref, y_ref, x_ref, g1_ref, sh2_ref, sc2_ref, n2_ref, wglu_ref, bglu_ref, wo1_ref, wo2_ref,
                x1_ref, h2_ref, o_scr, z_scr, x1_scr, *, n_slabs, tn):
    j = pl.program_id(1)
    rows = x_ref.shape[0] // n_slabs

    @pl.when(j == 0)
    def _():
        yg = _gelu(y_ref[...])
        z_scr[...] = (yg * _sigmoid(_dot(yg, wglu_ref[...]) + bglu_ref[...])).astype(BF16)
        o_scr[...] = o_ref[...].astype(BF16)

    mix = (jnp.dot(o_scr[...], wo1_ref[...], preferred_element_type=F32)
           + jnp.dot(z_scr[...], wo2_ref[...], preferred_element_type=F32))
    cols = pl.ds(pl.multiple_of(j * tn, tn), tn)
    for l in range(n_slabs):
        sl = slice(l * rows, (l + 1) * rows)
        x1 = x_ref[sl, :] + g1_ref[...] * mix[sl, :]
        x1_ref[sl, :] = x1
        x1_scr[sl, cols] = x1

    @pl.when(j == pl.num_programs(1) - 1)
    def _():
        for l in range(n_slabs):
            sl = slice(l * rows, (l + 1) * rows)
            h2 = _rms(x1_scr[sl, :]) * n2_ref[...] * (1.0 + sc2_ref[...]) + sh2_ref[...]
            h2_ref[sl, :] = h2.astype(BF16)


def _mix(o, y, x, mod, mod_row, norm2, w_glu_bf, b_glu, w_out_bf, *, tm, n_slabs, tn=512):
    r, d = x.shape
    gw = o.shape[1]
    sw = y.shape[1]
    nt = d // tn
    if mod.ndim == 3:
        mspec = lambda k, wid, jdep: pl.BlockSpec(
            (None, 1, wid), lambda i, j: (mod_row(i), 0, k * (d // wid) + (j if jdep else 0)))
    else:
        mspec = lambda k, wid, jdep: pl.BlockSpec(
            (mod.shape[0], wid), lambda i, j: (0, k * (d // wid) + (j if jdep else 0)))
    kern = functools.partial(_mix_kernel, n_slabs=n_slabs, tn=tn)
    return pl.pallas_call(
        kern,
        grid=(r // tm, nt),
        in_specs=[pl.BlockSpec((tm, gw), lambda i, j: (i, 0)),
                  pl.BlockSpec((tm, sw), lambda i, j: (i, 0)),
                  pl.BlockSpec((tm, tn), lambda i, j: (i, j)),
                  mspec(2, tn, True), mspec(3, d, False), mspec(4, d, False),
                  pl.BlockSpec((1, d), lambda i, j: (0, 0)),
                  pl.BlockSpec((sw, sw), lambda i, j: (0, 0)),
                  pl.BlockSpec((1, sw), lambda i, j: (0, 0)),
                  pl.BlockSpec((gw, tn), lambda i, j: (0, j)),
                  pl.BlockSpec((sw, tn), lambda i, j: (gw // sw, j))],
        out_specs=[pl.BlockSpec((tm, tn), lambda i, j: (i, j)),
                   pl.BlockSpec((tm, d), lambda i, j: (i, 0))],
        out_shape=[jax.ShapeDtypeStruct((r, d), F32), jax.ShapeDtypeStruct((r, d), BF16)],
        scratch_shapes=[pltpu.VMEM((tm, gw), BF16), pltpu.VMEM((tm, sw), BF16), pltpu.VMEM((tm, d), F32)],
        compiler_params=_cparams("arbitrary", "arbitrary"),
        name="mix",
    )(o, y, x, mod, mod, mod, norm2, w_glu_bf, b_glu, w_out_bf, w_out_bf)


def _conv3(up, ext, cw_ref, cb_ref, off1, off2):
    n = up.shape[0]
    return (cw_ref[2:3, :] * up + cw_ref[1:2, :] * ext[off1:off1 + n] + cw_ref[0:1, :] * ext[off2:off2 + n]
            + cb_ref[...])


def _ffn_tail(j, acc_scr, act, wd_ref):
    part = jnp.dot(act.astype(BF16), wd_ref[...], preferred_element_type=F32)

    @pl.when(j == 0)
    def _():
        acc_scr[...] = part

    @pl.when(j > 0)
    def _():
        acc_scr[...] += part


def _ffn_prompt_kernel(h_ref, halo_ref, x1_ref, g2_ref, fn_ref, wua_ref, wug_ref, cwa_ref, cwg_ref,
                       cba_ref, cbg_ref, wd_ref, y_ref, cs_ref, acc_scr, *, tiles_per_seq, halo):
    i, j = pl.program_id(0), pl.program_id(1)
    first = (i % tiles_per_seq) == 0
    h = h_ref[...]
    hh = halo_ref[...]
    tm = h.shape[0]

    def half(w_ref, cw_ref, cb_ref):
        up = jnp.dot(h, w_ref[...], preferred_element_type=F32)
        up_halo = jnp.where(first, 0.0, jnp.dot(hh, w_ref[...], preferred_element_type=F32))
        ext = jnp.concatenate([up_halo, up], axis=0)
        return _conv3(up, ext, cw_ref, cb_ref, halo - 1, halo - 2), up

    a, up_a = half(wua_ref, cwa_ref, cba_ref)
    g, up_g = half(wug_ref, cwg_ref, cbg_ref)
    for rr in range(2):
        row = tm - 2 + rr
        cs_ref[rr] = jnp.concatenate([up_a[row:row + 1], up_g[row:row + 1]], axis=0)
    _ffn_tail(j, acc_scr, _gelu(a) * g, wd_ref)

    @pl.when(j == pl.num_programs(1) - 1)
    def _():
        y_ref[...] = _rms(x1_ref[...] + g2_ref[...] * acc_scr[...]) * fn_ref[...]


def _ffn_prompt(h2, x1, mod, final_norm, w_up_bf, conv_w, conv_b, w_down_bf, *, b, l, tm, tf=512, halo=16):
    r, d = x1.shape
    dff = w_down_bf.shape[0]
    nf = dff // tf
    tps = l // tm
    kern = functools.partial(_ffn_prompt_kernel, tiles_per_seq=tps, halo=halo)
    return pl.pallas_call(
        kern,
        grid=(r // tm, nf),
        in_specs=[pl.BlockSpec((tm, d), lambda i, j: (i, 0)),
                  pl.BlockSpec((halo, d), lambda i, j: (jnp.maximum(i * (tm // halo) - 1, 0), 0)),
                  pl.BlockSpec((tm, d), lambda i, j: (i, 0)),
                  pl.BlockSpec((None, 1, d), lambda i, j: (i // tps, 0, 5)),
                  pl.BlockSpec((1, d), lambda i, j: (0, 0)),
                  pl.BlockSpec((d, tf), lambda i, j: (0, j)),
                  pl.BlockSpec((d, tf), lambda i, j: (0, nf + j)),
                  pl.BlockSpec((3, tf), lambda i, j: (0, j)),
                  pl.BlockSpec((3, tf), lambda i, j: (0, nf + j)),
                  pl.BlockSpec((1, tf), lambda i, j: (0, j)),
                  pl.BlockSpec((1, tf), lambda i, j: (0, nf + j)),
                  pl.BlockSpec((tf, d), lambda i, j: (j, 0))],
        out_specs=[pl.BlockSpec((tm, d), lambda i, j: (i, 0)),
                   pl.BlockSpec((None, 2, 2, tf), lambda i, j: (i, 0, 0, j))],
        out_shape=[jax.ShapeDtypeStruct((r, d), F32), jax.ShapeDtypeStruct((r // tm, 2, 2, dff), F32)],
        scratch_shapes=[pltpu.VMEM((tm, d), F32)],
        compiler_params=_cparams("arbitrary", "arbitrary"),
        name="ffn_prompt",
    )(h2, h2, x1, mod, final_norm, w_up_bf, w_up_bf, conv_w, conv_w, conv_b, conv_b, w_down_bf)


def _ffn_sample_kernel(h_ref, x1_ref, g2_ref, fn_ref, sta_ref, stg_ref, wua_ref, wug_ref, cwa_ref, cwg_ref,
                       cba_ref, cbg_ref, wd_ref, y_ref, csa_ref, csg_ref, acc_scr, *, nb, seq):
    j = pl.program_id(0)
    h = h_ref[...]

    def half(w_ref, st_ref, cw_ref, cb_ref, cs_ref):
        up = jnp.dot(h, w_ref[...], preferred_element_type=F32)
        ext = jnp.concatenate([st_ref[...], up], axis=0)
        cs_ref[...] = up[(seq - 2) * nb:]
        return _conv3(up, ext, cw_ref, cb_ref, nb, 0)

    a = half(wua_ref, sta_ref, cwa_ref, cba_ref, csa_ref)
    g = half(wug_ref, stg_ref, cwg_ref, cbg_ref, csg_ref)
    _ffn_tail(j, acc_scr, _gelu(a) * g, wd_ref)

    @pl.when(j == pl.num_programs(0) - 1)
    def _():
        for l in range(seq):
            sl = slice(l * nb, (l + 1) * nb)
            y_ref[sl, :] = _rms(x1_ref[sl, :] + g2_ref[...] * acc_scr[sl, :]) * fn_ref[...]


def _ffn_sample(h2, x1, mod, final_norm, st, w_up_bf, conv_w, conv_b, w_down_bf, *, nb, seq, tf=512):
    r, d = x1.shape
    dff = w_down_bf.shape[0]
    nf = dff // tf
    kern = functools.partial(_ffn_sample_kernel, nb=nb, seq=seq)
    return pl.pallas_call(
        kern,
        grid=(nf,),
        in_specs=[pl.BlockSpec((r, d), lambda j: (0, 0)),
                  pl.BlockSpec((r, d), lambda j: (0, 0)),
                  pl.BlockSpec((nb, d), lambda j: (0, 5)),
                  pl.BlockSpec((1, d), lambda j: (0, 0)),
                  pl.BlockSpec((2 * nb, tf), lambda j: (0, j)),
                  pl.BlockSpec((2 * nb, tf), lambda j: (0, nf + j)),
                  pl.BlockSpec((d, tf), lambda j: (0, j)),
                  pl.BlockSpec((d, tf), lambda j: (0, nf + j)),
                  pl.BlockSpec((3, tf), lambda j: (0, j)),
                  pl.BlockSpec((3, tf), lambda j: (0, nf + j)),
                  pl.BlockSpec((1, tf), lambda j: (0, j)),
                  pl.BlockSpec((1, tf), lambda j: (0, nf + j)),
                  pl.BlockSpec((tf, d), lambda j: (j, 0))],
        out_specs=[pl.BlockSpec((r, d), lambda j: (0, 0)),
                   pl.BlockSpec((2 * nb, tf), lambda j: (0, j)),
                   pl.BlockSpec((2 * nb, tf), lambda j: (0, j))],
        out_shape=[jax.ShapeDtypeStruct((r, d), F32), jax.ShapeDtypeStruct((2 * nb, dff), F32),
                   jax.ShapeDtypeStruct((2 * nb, dff), F32)],
        scratch_shapes=[pltpu.VMEM((r, d), F32)],
        compiler_params=_cparams("arbitrary"),
        name="ffn_sample",
    )(h2, x1, mod, final_norm, st, st, w_up_bf, w_up_bf, conv_w, conv_w, conv_b, conv_b, w_down_bf)


def _layer(xp, xs, cp, cs, st_gla, st_re, st_im, st_conv, w, final_norm):
    b, l, d = xp.shape
    nb, seq, _ = xs.shape
    _, h, dk, dv = st_gla.shape
    kw, gw = h * dk, h * dv
    g, p, ch = w['s5_b_re'].shape
    sw = g * ch
    rank = w['w_a2'].shape[0]
    dff = w['w_down'].shape[0]
    main_w = 2 * kw + 2 * gw
    row = lambda t: t.reshape(1, -1)

    w_in_bf = w['w_in'].astype(BF16)
    w_u_bf = w['w_in'][:, main_w + rank:].astype(BF16)
    w_a_bf = w['w_in'][:, main_w:main_w + rank].astype(BF16)
    w_glu_bf = w['w_glu'].astype(BF16)
    w_out_bf = w['w_out'].astype(BF16)
    w_up_bf = w['w_up'].astype(BF16)
    w_down_bf = w['w_down'].astype(BF16)

    n_c = b + nb
    n_c_pad = -(-n_c // 8) * 8
    c_all = jnp.concatenate([cp, cs, jnp.zeros((n_c_pad - n_c, d), F32)], axis=0)
    mod = _ada(c_all, w['w_ada'], row(w['b_ada']))
    mod_p = mod[:b].reshape(b, 1, 6 * d)
    mod_s = mod[b:n_c]

    xp2 = xp.reshape(b * l, d)
    xs2 = jnp.swapaxes(xs, 0, 1).reshape(seq * nb, d)
    tm_p = min(TM_PROMPT, l)
    tps = l // tm_p
    seq_of = lambda i: i // tps

    proj = functools.partial(_inproj, norm1=row(w['norm1']), w_in_bf=w_in_bf, w_u_bf=w_u_bf, w_a_bf=w_a_bf,
                             w_a2=w['w_a2'], b_a2=row(w['b_a2']), main_w=main_w)
    qkvg_p, u_p, la_p = proj(xp2, mod_p, seq_of, tm=tm_p, n_slabs=1)
    qkvg_s, u_s, la_s = proj(xs2, mod_s, None, tm=seq * nb, n_slabs=seq)

    gn = row(w['gla_norm'])
    o_p, gla_p = _gla_prompt(qkvg_p, la_p, gn, b=b, l=l, h=h, dk=dk, dv=dv)
    o_s, gla_s = _gla_sample(qkvg_s, la_s, gn, st_gla, nb=nb, seq=seq, h=h, dk=dk, dv=dv)

    s5w = (w['s5_lam_re'], w['s5_lam_im'], w['s5_log_step'], w['s5_b_re'], w['s5_b_im'],
           w['s5_c_re'], w['s5_c_im'], w['s5_d'])
    cp_ = math.gcd(l, S5_CHUNK)
    n_chunk = l // cp_
    m_p, wst_p, v_p, lc_p = _s5_prep(*s5w, c_mat=cp_, c_real=cp_)
    x5_p = u_p.reshape(b * n_chunk, cp_, g, ch).transpose(2, 0, 1, 3).reshape(g, b * n_chunk, cp_ * ch)
    y5_p, hf_p = _s5_prompt(x5_p, m_p, wst_p, v_p, lc_p, n_seq=b, n_chunk=n_chunk)
    y5_p = y5_p.reshape(g, b * n_chunk, cp_, ch).transpose(1, 2, 0, 3).reshape(b * l, sw)
    re_p = jnp.swapaxes(hf_p[:, :, :p], 0, 1)
    im_p = jnp.swapaxes(hf_p[:, :, p:], 0, 1)

    cs_ = S5_CHUNK_SAMPLE
    m_s, wst_s, v_s, lc_s = _s5_prep(*s5w, c_mat=cs_, c_real=seq)
    x5_s = u_s.reshape(seq, nb, g, ch).transpose(2, 1, 0, 3).reshape(g, nb, seq * ch)
    x5_s = jnp.pad(x5_s, ((0, 0), (0, 0), (0, (cs_ - seq) * ch)))
    h0_s = jnp.concatenate([jnp.swapaxes(st_re, 0, 1), jnp.swapaxes(st_im, 0, 1)], axis=-1)
    y5_s, hf_s = _s5_sample(x5_s, h0_s, m_s, wst_s, v_s, lc_s)
    y5_s = y5_s[:, :, :seq * ch].reshape(g, nb, seq, ch).transpose(2, 1, 0, 3).reshape(seq * nb, sw)
    re_s = jnp.swapaxes(hf_s[:, :, :p], 0, 1)
    im_s = jnp.swapaxes(hf_s[:, :, p:], 0, 1)

    mixer = functools.partial(_mix, norm2=row(w['norm2']), w_glu_bf=w_glu_bf, b_glu=row(w['b_glu']),
                              w_out_bf=w_out_bf)
    tm_mix = min(TM_MIX, l)
    x1_p, h2_p = mixer(o_p, y5_p, xp2, mod_p, lambda i: i // (l // tm_mix), tm=tm_mix, n_slabs=1)
    x1_s, h2_s = mixer(o_s, y5_s, xs2, mod_s, None, tm=seq * nb, n_slabs=seq)

    fn = row(final_norm)
    conv_w, conv_b = w['conv_w'], row(w['conv_b'])
    yp, cs_p = _ffn_prompt(h2_p, x1_p, mod_p, fn, w_up_bf, conv_w, conv_b, w_down_bf, b=b, l=l, tm=tm_p)
    conv_p = cs_p[tps - 1::tps].reshape(b, 2, 2 * dff)
    st_slab = jnp.swapaxes(st_conv, 0, 1).reshape(2 * nb, 2 * dff)
    ys, csa, csg = _ffn_sample(h2_s, x1_s, mod_s, fn, st_slab, w_up_bf, conv_w, conv_b, w_down_bf,
                               nb=nb, seq=seq)
    conv_s = jnp.swapaxes(jnp.concatenate([csa, csg], axis=-1).reshape(2, nb, 2 * dff), 0, 1)

    yp = yp.reshape(b, l, d)
    ys = jnp.swapaxes(ys.reshape(seq, nb, d), 0, 1)
    return yp, ys, (gla_p, re_p, im_p, conv_p), (gla_s, re_s, im_s, conv_s)


def kernel(x_prompt, x_sample, c_prompt, c_sample, state_gla, state_s5_re, state_s5_im, state_conv, w_ada, b_ada, norm1, w_in, w_a2, b_a2, gla_norm, s5_lam_re, s5_lam_im, s5_log_step, s5_b_re, s5_b_im, s5_c_re, s5_c_im, s5_d, w_glu, b_glu, w_out, norm2, w_up, conv_w, conv_b, w_down, final_norm):
    depth = w_ada.shape[0]
    assert depth == 1, "the final norm is fused into the last layer's FFN; only depth 1 is wired up"
    w = dict(w_ada=w_ada[0], b_ada=b_ada[0], norm1=norm1[0], w_in=w_in[0], w_a2=w_a2[0], b_a2=b_a2[0],
             gla_norm=gla_norm[0], s5_lam_re=s5_lam_re[0], s5_lam_im=s5_lam_im[0], s5_log_step=s5_log_step[0],
             s5_b_re=s5_b_re[0], s5_b_im=s5_b_im[0], s5_c_re=s5_c_re[0], s5_c_im=s5_c_im[0], s5_d=s5_d[0],
             w_glu=w_glu[0], b_glu=b_glu[0], w_out=w_out[0], norm2=norm2[0], w_up=w_up[0], conv_w=conv_w[0],
             conv_b=conv_b[0], w_down=w_down[0])
    yp, ys, sp, ss = _layer(x_prompt, x_sample, c_prompt, c_sample, state_gla[0], state_s5_re[0],
                            state_s5_im[0], state_conv[0], w, final_norm)
    stack = lambda t: t[None]
    return (yp, ys, stack(sp[0]), stack(sp[1]), stack(sp[2]), stack(sp[3]),
            stack(ss[0]), stack(ss[1]), stack(ss[2]), stack(ss[3]))
```

```python
import functools
import math

import jax
import jax.numpy as jnp
from jax import lax
from jax.experimental import pallas as pl
from jax.experimental.pallas import tpu as pltpu

F32 = jnp.float32
BF16 = jnp.bfloat16

NORM_EPS = 1e-6
GLA_TAU = 16.0
GLA_CHUNK = 64
GLA_SUB = 8
LOG2_E = 1.4426950408889634
GLA_SEQ_BLOCK = 512
LANES = 128
GLA_PAD = 128
S5_CHUNK = 16
S5_CHUNK_SAMPLE = 8
VMEM_LIMIT_BYTES = 56 * 1024 * 1024
TM_PROMPT = 512
TM_MIX = 256


def _cparams(*sem):
    return pltpu.CompilerParams(dimension_semantics=sem, vmem_limit_bytes=VMEM_LIMIT_BYTES)


def _dot(a, b):
    return jnp.dot(a.astype(BF16), b.astype(BF16), preferred_element_type=F32)


def _dot_nt(a, b):
    return lax.dot_general(a.astype(BF16), b.astype(BF16), (((1,), (1,)), ((), ())),
                           preferred_element_type=F32)


def _dot_f32(a, b):
    return jnp.dot(a, b, preferred_element_type=F32, precision=lax.Precision.HIGHEST)


def _rms(x):
    return x * lax.rsqrt(jnp.mean(x * x, axis=-1, keepdims=True) + NORM_EPS)


def _gelu(x):
    return 0.5 * x * (1.0 + jnp.tanh(math.sqrt(2.0 / math.pi) * (x + 0.044715 * (x * x * x))))


def _sigmoid(x):
    return 1.0 / (1.0 + jnp.exp(-x))


def _ada_kernel(c_ref, w_ref, b_ref, o_ref):
    c = c_ref[...]
    o_ref[...] = _dot(c * _sigmoid(c), w_ref[...]) + b_ref[...]


def _ada(c_all, w_ada, b_ada, tn=1024):
    m, d = c_all.shape
    n = w_ada.shape[1]
    return pl.pallas_call(
        _ada_kernel,
        grid=(n // tn,),
        in_specs=[pl.BlockSpec((m, d), lambda j: (0, 0)),
                  pl.BlockSpec((d, tn), lambda j: (0, j)),
                  pl.BlockSpec((1, tn), lambda j: (0, j))],
        out_specs=pl.BlockSpec((m, tn), lambda j: (0, j)),
        out_shape=jax.ShapeDtypeStruct((m, n), F32),
        compiler_params=_cparams("arbitrary"),
        name="ada",
    )(c_all, w_ada, b_ada)


def _inproj_kernel(x_ref, sh_ref, sc_ref, n1_ref, wm_ref, wu_ref, wa_ref, wa2_ref, ba2_ref,
                   qkvg_ref, u_ref, la_ref, h_scr, *, n_slabs, n_main):
    j = pl.program_id(1)
    rows = x_ref.shape[0] // n_slabs

    @pl.when(j == 0)
    def _():
        for l in range(n_slabs):
            sl = slice(l * rows, (l + 1) * rows)
            h = _rms(x_ref[sl, :]) * n1_ref[...] * (1.0 + sc_ref[...]) + sh_ref[...]
            h_scr[sl, :] = h.astype(BF16)
        a_lr = jnp.dot(h_scr[...], wa_ref[...], preferred_element_type=F32)
        z = _dot(a_lr, wa2_ref[...]) + ba2_ref[...]
        la_ref[...] = (jnp.minimum(z, 0.0) - jnp.log(1.0 + jnp.exp(-jnp.abs(z)))) / GLA_TAU

    @pl.when(j < n_main)
    def _():
        qkvg_ref[...] = jnp.dot(h_scr[...], wm_ref[...], preferred_element_type=F32)

    @pl.when(j >= n_main)
    def _():
        u_ref[...] = jnp.dot(h_scr[...], wu_ref[...], preferred_element_type=F32)


def _inproj(x, mod, mod_row, norm1, w_in_bf, w_u_bf, w_a_bf, w_a2, b_a2, *, tm, n_slabs, main_w, tn=512):
    r, d = x.shape
    n_main = main_w // tn
    n_u = w_u_bf.shape[1] // tn
    kw = w_a2.shape[1]
    rank = w_a_bf.shape[1]
    if mod.ndim == 3:
        mspec = lambda k: pl.BlockSpec((None, 1, d), lambda i, j: (mod_row(i), 0, k))
    else:
        mspec = lambda k: pl.BlockSpec((mod.shape[0], d), lambda i, j: (0, k))
    kern = functools.partial(_inproj_kernel, n_slabs=n_slabs, n_main=n_main)
    return pl.pallas_call(
        kern,
        grid=(r // tm, n_main + n_u),
        in_specs=[pl.BlockSpec((tm, d), lambda i, j: (i, 0)),
                  mspec(0), mspec(1),
                  pl.BlockSpec((1, d), lambda i, j: (0, 0)),
                  pl.BlockSpec((d, tn), lambda i, j: (0, jnp.minimum(j, n_main - 1))),
                  pl.BlockSpec((d, tn), lambda i, j: (0, jnp.maximum(j - n_main, 0))),
                  pl.BlockSpec((d, rank), lambda i, j: (0, 0)),
                  pl.BlockSpec((rank, kw), lambda i, j: (0, 0)),
                  pl.BlockSpec((1, kw), lambda i, j: (0, 0))],
        out_specs=[pl.BlockSpec((tm, tn), lambda i, j: (i, jnp.minimum(j, n_main - 1))),
                   pl.BlockSpec((tm, tn), lambda i, j: (i, jnp.maximum(j - n_main, 0))),
                   pl.BlockSpec((tm, kw), lambda i, j: (i, 0))],
        out_shape=[jax.ShapeDtypeStruct((r, main_w), F32),
                   jax.ShapeDtypeStruct((r, w_u_bf.shape[1]), F32),
                   jax.ShapeDtypeStruct((r, kw), F32)],
        scratch_shapes=[pltpu.VMEM((tm, d), BF16)],
        compiler_params=_cparams("arbitrary", "arbitrary"),
        name="inproj",
    )(x, mod, mod, norm1, w_in_bf, w_u_bf, w_a_bf, w_a2, b_a2)


def _gla_chunk(q, k, v, ga, states, *, h, sub):
    c = q.shape[0]
    dk, dv = q.shape[1] // h, v.shape[1] // h
    rows = lax.broadcasted_iota(jnp.int32, (c, c), 0)
    cols = lax.broadcasted_iota(jnp.int32, (c, c), 1)
    sums, widths = [rows >= cols], []
    w = sub
    while w < c:
        sums.append(cols <= (rows // (2 * w)) * (2 * w) + (w - 1))
        widths.append(w)
        w *= 2
    sums.append(cols >= 0)
    pmat = jnp.concatenate(sums, axis=0).astype(F32).astype(BF16)
    ga_hi = ga.astype(BF16)
    ga_lo = (ga - ga_hi.astype(F32)).astype(BF16)
    tot = (jnp.dot(pmat, ga_hi, preferred_element_type=F32)
           + jnp.dot(pmat, ga_lo, preferred_element_type=F32))
    tot = tot * LOG2_E
    cum = tot[:c]
    last = tot[(len(sums) - 1) * c:]

    row_w = lax.broadcasted_iota(jnp.int32, q.shape, 0)
    lane_o = lax.broadcasted_iota(jnp.int32, (c, LANES), 1)
    by_off = [jnp.zeros((c, LANES), F32) for _ in range(h)]
    for off in range(sub):
        k_sh = k if off == 0 else pltpu.roll(k, off, axis=0)
        c_sh = cum if off == 0 else pltpu.roll(cum, off, axis=0)
        prod = q * k_sh * jnp.exp2(jnp.minimum(cum - c_sh, 0.0))
        for hh in range(h):
            col = jnp.sum(prod[:, hh * dk:(hh + 1) * dk], axis=-1, keepdims=True)
            by_off[hh] = jnp.where(lane_o == c - 1 - off, col, by_off[hh])
    same_sub = rows // sub == cols // sub
    att = []
    for hh in range(h):
        moved = pltpu.roll(by_off[hh], LANES - (c - 1), axis=1, stride=1, stride_axis=0)
        att.append(jnp.where(same_sub, moved[:, :c], 0.0))
    for lvl, w in enumerate(widths):
        ref = tot[(lvl + 1) * c:(lvl + 2) * c]
        odd = (row_w // w) % 2 == 1
        x = jnp.where(odd, q, k) * jnp.exp2(jnp.where(odd, cum - ref, ref - cum))
        pair = (rows // (2 * w) == cols // (2 * w)) & ((rows // w) % 2 == 1) & ((cols // w) % 2 == 0)
        for hh in range(h):
            xh = x[:, hh * dk:(hh + 1) * dk]
            att[hh] = att[hh] + jnp.where(pair, _dot_nt(xh, xh), 0.0)

    q_in = q * jnp.exp2(cum)
    k_out = k * jnp.exp2(last - cum)
    e_last = jnp.exp2(last[0:1])
    tail_row = lax.broadcasted_iota(jnp.int32, (GLA_PAD - c, dk), 0)
    outs, new_states = [], []
    for hh in range(h):
        ck, cv = slice(hh * dk, (hh + 1) * dk), slice(hh * dv, (hh + 1) * dv)
        outs.append(_dot(att[hh], v[:, cv]) + _dot(q_in[:, ck], states[hh]))
        tail = jnp.where(tail_row == 0, e_last[:, ck], 0.0)
        kt = jnp.concatenate([k_out[:, ck], tail], axis=0).T
        new_states.append(states[hh] * kt[:, c:c + 1] + _dot(kt[:, :c], v[:, cv]))
    return outs, new_states


def _gla_finish(o, g, gn):
    return _rms(o) * gn * (g * _sigmoid(g))


def _gla_prompt_kernel(q_ref, k_ref, v_ref, g_ref, la_ref, gn_ref, o_ref, s_ref, s_scr, *, h, dk, dv, chunk, sub,
                       scale):
    blk = pl.program_id(1)

    @pl.when(blk == 0)
    def _():
        s_scr[...] = jnp.zeros_like(s_scr)

    def body(n, carry):
        r = pl.ds(pl.multiple_of(n * chunk, chunk), chunk)
        outs, new_states = _gla_chunk(q_ref[r, :] * scale, k_ref[r, :], v_ref[r, :], la_ref[r, :],
                                      [s_scr[hh] for hh in range(h)], h=h, sub=sub)
        for hh in range(h):
            cv = slice(hh * dv, (hh + 1) * dv)
            s_scr[hh] = new_states[hh]
            o_ref[r, cv] = _gla_finish(outs[hh], g_ref[r, cv], gn_ref[...])
        return carry

    lax.fori_loop(0, q_ref.shape[0] // chunk, body, 0)

    @pl.when(blk == pl.num_programs(1) - 1)
    def _():
        s_ref[...] = s_scr[...]


def _gla_prompt(qkvg, la, gla_norm, *, b, l, h, dk, dv):
    kw, gw = h * dk, h * dv
    chunk = math.gcd(l, GLA_CHUNK)
    sub = math.gcd(chunk, GLA_SUB)
    lb = math.gcd(l, GLA_SEQ_BLOCK)
    nlb = l // lb
    kern = functools.partial(_gla_prompt_kernel, h=h, dk=dk, dv=dv, chunk=chunk, sub=sub, scale=dk ** -0.5)
    return pl.pallas_call(
        kern,
        grid=(b, nlb),
        in_specs=[pl.BlockSpec((lb, kw), lambda i, j: (i * nlb + j, 0)),
                  pl.BlockSpec((lb, kw), lambda i, j: (i * nlb + j, 1)),
                  pl.BlockSpec((lb, gw), lambda i, j: (i * nlb + j, 2 * kw // gw)),
                  pl.BlockSpec((lb, gw), lambda i, j: (i * nlb + j, 2 * kw // gw + 1)),
                  pl.BlockSpec((lb, kw), lambda i, j: (i * nlb + j, 0)),
                  pl.BlockSpec((1, dv), lambda i, j: (0, 0))],
        out_specs=[pl.BlockSpec((lb, gw), lambda i, j: (i * nlb + j, 0)),
                   pl.BlockSpec((None, h, dk, dv), lambda i, j: (i, 0, 0, 0))],
        out_shape=[jax.ShapeDtypeStruct((b * l, gw), F32),
                   jax.ShapeDtypeStruct((b, h, dk, dv), F32)],
        scratch_shapes=[pltpu.VMEM((h, dk, dv), F32)],
        compiler_params=_cparams("arbitrary", "arbitrary"),
        name="gla_prompt",
    )(qkvg, qkvg, qkvg, qkvg, la, gla_norm)


def _gla_sample_kernel(qkvg_ref, la_ref, gn_ref, s0_ref, o_ref, s_ref, *, nb, seq, h, dk, dv, pad, scale):
    bt = s0_ref.shape[0]
    kw, gw = h * dk, h * dv
    in_w = 2 * kw + 2 * gw
    i = pl.program_id(0)

    def rows_of(b, width, col):
        per_row = width // LANES
        return pl.ds(b * per_row + col // LANES, seq, stride=nb * per_row)

    def gather(ref, b, width, col, n_col):
        parts = [ref[rows_of(b, width, col + c), :] for c in range(0, n_col, LANES)]
        x = parts[0] if len(parts) == 1 else jnp.concatenate(parts, axis=1)
        return jnp.concatenate([x, jnp.zeros((pad - seq, n_col), F32)], axis=0)

    def body(bb, carry):
        b = i * bt + bb
        q = gather(qkvg_ref, b, in_w, 0, kw) * scale
        k = gather(qkvg_ref, b, in_w, kw, kw)
        v = gather(qkvg_ref, b, in_w, 2 * kw, gw)
        g = gather(qkvg_ref, b, in_w, 2 * kw + gw, gw)
        ga = gather(la_ref, b, kw, 0, kw)
        outs, new_states = _gla_chunk(q, k, v, ga, [s0_ref[bb, hh] for hh in range(h)], h=h, sub=pad)
        for hh in range(h):
            s_ref[bb, hh] = new_states[hh]
            res = _gla_finish(outs[hh], g[:, hh * dv:(hh + 1) * dv], gn_ref[...])
            for c in range(0, dv, LANES):
                o_ref[rows_of(b, gw, hh * dv + c), :] = res[:seq, c:c + LANES]
        return carry

    lax.fori_loop(0, bt, body, 0, unroll=4)


def _gla_sample(qkvg, la, gla_norm, s0, *, nb, seq, h, dk, dv, bt=8):
    gw = h * dv
    pad = 8
    assert dk % LANES == 0 and dv % LANES == 0
    qkvg, la = qkvg.reshape(-1, LANES), la.reshape(-1, LANES)
    o_rows = nb * seq * gw // LANES
    kern = functools.partial(_gla_sample_kernel, nb=nb, seq=seq, h=h, dk=dk, dv=dv, pad=pad, scale=dk ** -0.5)
    o, s_new = pl.pallas_call(
        kern,
        grid=(nb // bt,),
        in_specs=[pl.BlockSpec(qkvg.shape, lambda i: (0, 0)),
                  pl.BlockSpec(la.shape, lambda i: (0, 0)),
                  pl.BlockSpec((1, dv), lambda i: (0, 0)),
                  pl.BlockSpec((bt, h, dk, dv), lambda i: (i, 0, 0, 0))],
        out_specs=[pl.BlockSpec((o_rows, LANES), lambda i: (0, 0)),
                   pl.BlockSpec((bt, h, dk, dv), lambda i: (i, 0, 0, 0))],
        out_shape=[jax.ShapeDtypeStruct((o_rows, LANES), F32),
                   jax.ShapeDtypeStruct((nb, h, dk, dv), F32)],
        compiler_params=_cparams("arbitrary"),
        name="gla_sample",
    )(qkvg, la, gla_norm, s0)
    return o.reshape(nb * seq, gw), s_new


def _s5_prep_kernel(lamc_ref, lamr_ref, ls_ref, bt_re_ref, bt_im_ref, btile_re_ref, btile_im_ref,
                    ctile_re_ref, ctile_im_ref, d_ref, m_ref, wst_ref, v_ref, lc_ref, *, ch, c_mat, c_real):
    w = ch * c_mat
    p = lamc_ref.shape[1]
    lane_m = lax.broadcasted_iota(jnp.int32, (p, LANES), 1)
    m_f = jnp.where(lane_m <= c_mat, lane_m, 0).astype(F32)
    sel = lax.broadcasted_iota(jnp.int32, (LANES, w), 0)
    tau = lax.broadcasted_iota(jnp.int32, (LANES, w), 1) // ch
    spread_pw = (sel == tau).astype(F32)
    spread_pv = (sel == tau + 1).astype(F32)
    spread_ps = (sel == jnp.maximum(c_real - 1 - tau, 0)).astype(F32)
    rr = lax.broadcasted_iota(jnp.int32, (ch, w), 0)
    cc = lax.broadcasted_iota(jnp.int32, (ch, w), 1)

    def bbar_coef(lam_re, lam_im, lb_re, lb_im):
        den = lam_re * lam_re + lam_im * lam_im
        x, y = lb_re - 1.0, lb_im
        return (x * lam_re + y * lam_im) / den, (y * lam_re - x * lam_im) / den

    for gi in range(m_ref.shape[0]):
        dt = jnp.exp(ls_ref[gi])
        lam_re, lam_im = lamc_ref[gi, :, 0:1], lamc_ref[gi, :, 1:2]
        mag = jnp.exp(lam_re * dt * m_f)
        t_re, t_im = mag * jnp.cos(lam_im * dt * m_f), mag * jnp.sin(lam_im * dt * m_f)
        cf_re, cf_im = bbar_coef(lam_re, lam_im, t_re[:, 1:2], t_im[:, 1:2])

        ct_re, ct_im = ctile_re_ref[gi], ctile_im_ref[gi]
        pw_re, pw_im = _dot_f32(t_re, spread_pw), _dot_f32(t_im, spread_pw)
        cl_re = ct_re * pw_re - ct_im * pw_im
        cl_im = ct_re * pw_im + ct_im * pw_re
        pv_re, pv_im = _dot_f32(t_re, spread_pv), _dot_f32(t_im, spread_pv)
        v_ref[gi, 0:p, :] = (ct_re * pv_re - ct_im * pv_im).astype(v_ref.dtype)
        v_ref[gi, p:2 * p, :] = (-(ct_re * pv_im + ct_im * pv_re)).astype(v_ref.dtype)

        bb_re = cf_re * btile_re_ref[gi] - cf_im * btile_im_ref[gi]
        bb_im = cf_re * btile_im_ref[gi] + cf_im * btile_re_ref[gi]
        ps_re, ps_im = _dot_f32(t_re, spread_ps), _dot_f32(t_im, spread_ps)
        wst_ref[gi, 0:p, :] = (bb_re * ps_re - bb_im * ps_im).astype(wst_ref.dtype)
        wst_ref[gi, p:2 * p, :] = (bb_re * ps_im + bb_im * ps_re).astype(wst_ref.dtype)

        lr_re, lr_im = lamr_ref[gi, 0:1, :], lamr_ref[gi, 1:2, :]
        ar, tr = lr_re * dt, lr_im * dt
        ea = jnp.exp(ar)
        rf_re, rf_im = bbar_coef(lr_re, lr_im, ea * jnp.cos(tr), ea * jnp.sin(tr))
        bbt_re = rf_re * bt_re_ref[gi] - rf_im * bt_im_ref[gi]
        bbt_im = rf_re * bt_im_ref[gi] + rf_im * bt_re_ref[gi]
        kcat = _dot_f32(bbt_re, cl_re) - _dot_f32(bbt_im, cl_im)
        kcat = kcat + jnp.where(rr == cc, d_ref[gi], 0.0)
        for i in range(c_mat):
            shifted = kcat if i == 0 else pltpu.roll(kcat, ch * i, axis=1)
            m_ref[gi, i * ch:(i + 1) * ch, :] = jnp.where(cc >= ch * i, shifted, 0.0).astype(m_ref.dtype)

        ec = jnp.exp(ar * float(c_real))
        lc_ref[gi, 0:1, :] = ec * jnp.cos(tr * float(c_real))
        lc_ref[gi, 1:2, :] = ec * jnp.sin(tr * float(c_real))


def _s5_prep(lam_re, lam_im, log_step, b_re, b_im, c_re, c_im, d_skip, *, c_mat, c_real):
    g, p, ch = b_re.shape
    w = ch * c_mat
    gb = LANES // ch
    lam_col = jnp.stack([lam_re, lam_im], axis=-1)
    lam_row = jnp.stack([lam_re, lam_im], axis=1)
    ls = log_step.reshape(g, 1, 1)
    tile = lambda t: jnp.tile(t, (1, 1, c_mat))
    bt = lambda t: jnp.swapaxes(t, 1, 2)
    d_row = jnp.pad(d_skip.reshape(g, 1, ch), ((0, 0), (0, 0), (0, w - ch)))
    spec3 = lambda s: pl.BlockSpec((gb,) + s, lambda i: (i, 0, 0))
    kern = functools.partial(_s5_prep_kernel, ch=ch, c_mat=c_mat, c_real=c_real)
    return pl.pallas_call(
        kern,
        grid=(g // gb,),
        in_specs=[spec3((p, 2)), spec3((2, p)), spec3((1, 1)), spec3((ch, p)), spec3((ch, p)),
                  spec3((p, w)), spec3((p, w)), spec3((p, w)), spec3((p, w)), spec3((1, w))],
        out_specs=[spec3((w, w)), spec3((2 * p, w)), spec3((2 * p, w)), spec3((2, p))],
        out_shape=[jax.ShapeDtypeStruct((g, w, w), BF16), jax.ShapeDtypeStruct((g, 2 * p, w), BF16),
                   jax.ShapeDtypeStruct((g, 2 * p, w), BF16), jax.ShapeDtypeStruct((g, 2, p), F32)],
        compiler_params=_cparams("arbitrary"),
        name="s5_prep",
    )(lam_col, lam_row, ls, bt(b_re), bt(b_im), tile(b_re), tile(b_im),
      tile(jnp.swapaxes(c_re, 1, 2)), tile(jnp.swapaxes(c_im, 1, 2)), d_row)


def _cmul_rows(z, lr, li):
    p = lr.shape[1]
    coef_a = jnp.concatenate([lr, lr], axis=1)
    coef_b = jnp.concatenate([-li, li], axis=1)
    return z * coef_a + pltpu.roll(z, p, axis=1) * coef_b


def _regroup(parts, gi, ch):
    return jnp.concatenate([t[:, gi * ch:(gi + 1) * ch] for t in parts], axis=1)


def _s5_prompt_kernel(u_ref, m_ref, wst_ref, v_ref, lc_ref, y_ref, hre_ref, him_ref, *, ch, c, n_seq, n_chunk):
    gb = m_ref.shape[0]
    r = n_seq * n_chunk
    p = lc_ref.shape[2]
    toks = [u_ref[pl.ds(i, r, stride=c), :] for i in range(c)]
    pos = lax.broadcasted_iota(jnp.int32, (r, 2 * p), 0) % n_chunk
    ys, h_re, h_im = [], [], []
    for gi in range(gb):
        x = _regroup(toks, gi, ch).astype(BF16)
        z = lax.dot_general(x, wst_ref[gi], (((1,), (1,)), ((), ())), preferred_element_type=F32)
        lr, li = lc_ref[gi, 0:1, :], lc_ref[gi, 1:2, :]
        d = 1
        while d < n_chunk:
            zs = jnp.where(pos >= d, pltpu.roll(z, d, axis=0), 0.0)
            z = z + _cmul_rows(zs, lr, li)
            lr, li = lr * lr - li * li, 2.0 * lr * li
            d *= 2
        h_in = jnp.where(pos >= 1, pltpu.roll(z, 1, axis=0), 0.0)
        ys.append(jnp.dot(x, m_ref[gi], preferred_element_type=F32)
                  + jnp.dot(h_in.astype(BF16), v_ref[gi], preferred_element_type=F32))
        hf = jnp.concatenate([z[(s + 1) * n_chunk - 1:(s + 1) * n_chunk] for s in range(n_seq)], axis=0)
        h_re.append(hf[:, :p])
        h_im.append(hf[:, p:])
    for j in range(c):
        y_ref[pl.ds(j, r, stride=c), :] = _regroup(ys, j, ch)
    hre_ref[...] = jnp.concatenate(h_re, axis=1)
    him_ref[...] = jnp.concatenate(h_im, axis=1)


def _s5_prompt(u, m, wst, v, lc, *, n_seq, n_chunk, c, ch):
    t, sw = u.shape
    g, w, _ = m.shape
    p2 = wst.shape[1]
    p = p2 // 2
    gb = LANES // ch
    spec3 = lambda s: pl.BlockSpec((gb,) + s, lambda i: (i, 0, 0))
    kern = functools.partial(_s5_prompt_kernel, ch=ch, c=c, n_seq=n_seq, n_chunk=n_chunk)
    return pl.pallas_call(
        kern,
        grid=(g // gb,),
        in_specs=[pl.BlockSpec((t, LANES), lambda i: (0, i)),
                  spec3((w, w)), spec3((p2, w)), spec3((p2, w)), spec3((2, p))],
        out_specs=[pl.BlockSpec((t, LANES), lambda i: (0, i)),
                   pl.BlockSpec((n_seq, gb * p), lambda i: (0, i)),
                   pl.BlockSpec((n_seq, gb * p), lambda i: (0, i))],
        out_shape=[jax.ShapeDtypeStruct((t, sw), F32), jax.ShapeDtypeStruct((n_seq, g * p), F32),
                   jax.ShapeDtypeStruct((n_seq, g * p), F32)],
        compiler_params=_cparams("arbitrary"),
        name="s5_prompt",
    )(u, m, wst, v, lc)


def _s5_sample_kernel(u_ref, sre_ref, sim_ref, m_ref, wst_ref, v_ref, lc_ref, y_ref, hre_ref, him_ref, *,
                      ch, seq, nb):
    gb = m_ref.shape[0]
    p = lc_ref.shape[2]
    w = m_ref.shape[1]
    toks = [u_ref[l * nb:(l + 1) * nb, :] for l in range(seq)]
    ys, h_re, h_im = [], [], []
    for gi in range(gb):
        x = jnp.concatenate([_regroup(toks, gi, ch), jnp.zeros((nb, w - seq * ch), F32)], axis=1).astype(BF16)
        h0 = jnp.concatenate([sre_ref[:, gi * p:(gi + 1) * p], sim_ref[:, gi * p:(gi + 1) * p]], axis=1)
        ys.append(jnp.dot(x, m_ref[gi], preferred_element_type=F32)
                  + jnp.dot(h0.astype(BF16), v_ref[gi], preferred_element_type=F32))
        hf = (_cmul_rows(h0, lc_ref[gi, 0:1, :], lc_ref[gi, 1:2, :])
              + lax.dot_general(x, wst_ref[gi], (((1,), (1,)), ((), ())), preferred_element_type=F32))
        h_re.append(hf[:, :p])
        h_im.append(hf[:, p:])
    for l in range(seq):
        y_ref[l * nb:(l + 1) * nb, :] = _regroup(ys, l, ch)
    hre_ref[...] = jnp.concatenate(h_re, axis=1)
    him_ref[...] = jnp.concatenate(h_im, axis=1)


def _s5_sample(u, st_re, st_im, m, wst, v, lc, *, seq, nb, ch):
    t, sw = u.shape
    g, w, _ = m.shape
    p2 = wst.shape[1]
    p = p2 // 2
    gb = LANES // ch
    spec3 = lambda s: pl.BlockSpec((gb,) + s, lambda i: (i, 0, 0))
    sspec = pl.BlockSpec((nb, gb * p), lambda i: (0, i))
    kern = functools.partial(_s5_sample_kernel, ch=ch, seq=seq, nb=nb)
    return pl.pallas_call(
        kern,
        grid=(g // gb,),
        in_specs=[pl.BlockSpec((t, LANES), lambda i: (0, i)), sspec, sspec,
                  spec3((w, w)), spec3((p2, w)), spec3((p2, w)), spec3((2, p))],
        out_specs=[pl.BlockSpec((t, LANES), lambda i: (0, i)), sspec, sspec],
        out_shape=[jax.ShapeDtypeStruct((t, sw), F32), jax.ShapeDtypeStruct((nb, g * p), F32),
                   jax.ShapeDtypeStruct((nb, g * p), F32)],
        compiler_params=_cparams("arbitrary"),
        name="s5_sample",
    )(u, st_re, st_im, m, wst, v, lc)


def _mix_kernel(o_ref, y_ref, x_ref, g1_ref, sh2_ref, sc2_ref, n2_ref, wglu_ref, bglu_ref, wo1_ref, wo2_ref,
                x1_ref, h2_ref, o_scr, z_scr, x1_scr, *, n_slabs, tn):
    j = pl.program_id(1)
    rows = x_ref.shape[0] // n_slabs

    @pl.when(j == 0)
    def _():
        yg = _gelu(y_ref[...])
        z_scr[...] = (yg * _sigmoid(_dot(yg, wglu_ref[...]) + bglu_ref[...])).astype(BF16)
        o_scr[...] = o_ref[...].astype(BF16)

    mix = (jnp.dot(o_scr[...], wo1_ref[...], preferred_element_type=F32)
           + jnp.dot(z_scr[...], wo2_ref[...], preferred_element_type=F32))
    cols = pl.ds(pl.multiple_of(j * tn, tn), tn)
    for l in range(n_slabs):
        sl = slice(l * rows, (l + 1) * rows)
        x1 = x_ref[sl, :] + g1_ref[...] * mix[sl, :]
        x1_ref[sl, :] = x1
        x1_scr[sl, cols] = x1

    @pl.when(j == pl.num_programs(1) - 1)
    def _():
        for l in range(n_slabs):
            sl = slice(l * rows, (l + 1) * rows)
            h2 = _rms(x1_scr[sl, :]) * n2_ref[...] * (1.0 + sc2_ref[...]) + sh2_ref[...]
            h2_ref[sl, :] = h2.astype(BF16)


def _mix(o, y, x, mod, mod_row, norm2, w_glu_bf, b_glu, w_out_bf, *, tm, n_slabs, tn=512):
    r, d = x.shape
    gw = o.shape[1]
    sw = y.shape[1]
    nt = d // tn
    if mod.ndim == 3:
        mspec = lambda k, wid, jdep: pl.BlockSpec(
            (None, 1, wid), lambda i, j: (mod_row(i), 0, k * (d // wid) + (j if jdep else 0)))
    else:
        mspec = lambda k, wid, jdep: pl.BlockSpec(
            (mod.shape[0], wid), lambda i, j: (0, k * (d // wid) + (j if jdep else 0)))
    kern = functools.partial(_mix_kernel, n_slabs=n_slabs, tn=tn)
    return pl.pallas_call(
        kern,
        grid=(r // tm, nt),
        in_specs=[pl.BlockSpec((tm, gw), lambda i, j: (i, 0)),
                  pl.BlockSpec((tm, sw), lambda i, j: (i, 0)),
                  pl.BlockSpec((tm, tn), lambda i, j: (i, j)),
                  mspec(2, tn, True), mspec(3, d, False), mspec(4, d, False),
                  pl.BlockSpec((1, d), lambda i, j: (0, 0)),
                  pl.BlockSpec((sw, sw), lambda i, j: (0, 0)),
                  pl.BlockSpec((1, sw), lambda i, j: (0, 0)),
                  pl.BlockSpec((gw, tn), lambda i, j: (0, j)),
                  pl.BlockSpec((sw, tn), lambda i, j: (gw // sw, j))],
        out_specs=[pl.BlockSpec((tm, tn), lambda i, j: (i, j)),
                   pl.BlockSpec((tm, d), lambda i, j: (i, 0))],
        out_shape=[jax.ShapeDtypeStruct((r, d), F32), jax.ShapeDtypeStruct((r, d), BF16)],
        scratch_shapes=[pltpu.VMEM((tm, gw), BF16), pltpu.VMEM((tm, sw), BF16), pltpu.VMEM((tm, d), F32)],
        compiler_params=_cparams("arbitrary", "arbitrary"),
        name="mix",
    )(o, y, x, mod, mod, mod, norm2, w_glu_bf, b_glu, w_out_bf, w_out_bf)


def _conv3(up, ext, cw_ref, cb_ref, off1, off2):
    n = up.shape[0]
    return (cw_ref[2:3, :] * up + cw_ref[1:2, :] * ext[off1:off1 + n] + cw_ref[0:1, :] * ext[off2:off2 + n]
            + cb_ref[...])


def _ffn_tail(j, acc_scr, act, wd_ref):
    part = jnp.dot(act.astype(BF16), wd_ref[...], preferred_element_type=F32)

    @pl.when(j == 0)
    def _():
        acc_scr[...] = part

    @pl.when(j > 0)
    def _():
        acc_scr[...] += part


def _ffn_prompt_kernel(h_ref, halo_ref, x1_ref, g2_ref, fn_ref, wua_ref, wug_ref, cwa_ref, cwg_ref,
                       cba_ref, cbg_ref, wd_ref, y_ref, cs_ref, acc_scr, *, tiles_per_seq, halo):
    i, j = pl.program_id(0), pl.program_id(1)
    first = (i % tiles_per_seq) == 0
    h = h_ref[...]
    hh = halo_ref[...]
    tm = h.shape[0]

    def half(w_ref, cw_ref, cb_ref):
        up = jnp.dot(h, w_ref[...], preferred_element_type=F32)
        up_halo = jnp.where(first, 0.0, jnp.dot(hh, w_ref[...], preferred_element_type=F32))
        ext = jnp.concatenate([up_halo, up], axis=0)
        return _conv3(up, ext, cw_ref, cb_ref, halo - 1, halo - 2), up

    a, up_a = half(wua_ref, cwa_ref, cba_ref)
    g, up_g = half(wug_ref, cwg_ref, cbg_ref)
    for rr in range(2):
        row = tm - 2 + rr
        cs_ref[rr] = jnp.concatenate([up_a[row:row + 1], up_g[row:row + 1]], axis=0)
    _ffn_tail(j, acc_scr, _gelu(a) * g, wd_ref)

    @pl.when(j == pl.num_programs(1) - 1)
    def _():
        y_ref[...] = _rms(x1_ref[...] + g2_ref[...] * acc_scr[...]) * fn_ref[...]


def _ffn_prompt(h2, x1, mod, final_norm, w_up_bf, conv_w, conv_b, w_down_bf, *, b, l, tm, tf=512, halo=16):
    r, d = x1.shape
    dff = w_down_bf.shape[0]
    nf = dff // tf
    tps = l // tm
    kern = functools.partial(_ffn_prompt_kernel, tiles_per_seq=tps, halo=halo)
    return pl.pallas_call(
        kern,
        grid=(r // tm, nf),
        in_specs=[pl.BlockSpec((tm, d), lambda i, j: (i, 0)),
                  pl.BlockSpec((halo, d), lambda i, j: (jnp.maximum(i * (tm // halo) - 1, 0), 0)),
                  pl.BlockSpec((tm, d), lambda i, j: (i, 0)),
                  pl.BlockSpec((None, 1, d), lambda i, j: (i // tps, 0, 5)),
                  pl.BlockSpec((1, d), lambda i, j: (0, 0)),
                  pl.BlockSpec((d, tf), lambda i, j: (0, j)),
                  pl.BlockSpec((d, tf), lambda i, j: (0, nf + j)),
                  pl.BlockSpec((3, tf), lambda i, j: (0, j)),
                  pl.BlockSpec((3, tf), lambda i, j: (0, nf + j)),
                  pl.BlockSpec((1, tf), lambda i, j: (0, j)),
                  pl.BlockSpec((1, tf), lambda i, j: (0, nf + j)),
                  pl.BlockSpec((tf, d), lambda i, j: (j, 0))],
        out_specs=[pl.BlockSpec((tm, d), lambda i, j: (i, 0)),
                   pl.BlockSpec((None, 2, 2, tf), lambda i, j: (i, 0, 0, j))],
        out_shape=[jax.ShapeDtypeStruct((r, d), F32), jax.ShapeDtypeStruct((r // tm, 2, 2, dff), F32)],
        scratch_shapes=[pltpu.VMEM((tm, d), F32)],
        compiler_params=_cparams("arbitrary", "arbitrary"),
        name="ffn_prompt",
    )(h2, h2, x1, mod, final_norm, w_up_bf, w_up_bf, conv_w, conv_w, conv_b, conv_b, w_down_bf)


def _ffn_sample_kernel(h_ref, x1_ref, g2_ref, fn_ref, sta_ref, stg_ref, wua_ref, wug_ref, cwa_ref, cwg_ref,
                       cba_ref, cbg_ref, wd_ref, y_ref, csa_ref, csg_ref, acc_scr, *, nb, seq):
    j = pl.program_id(0)
    h = h_ref[...]

    def half(w_ref, st_ref, cw_ref, cb_ref, cs_ref):
        up = jnp.dot(h, w_ref[...], preferred_element_type=F32)
        ext = jnp.concatenate([st_ref[...], up], axis=0)
        cs_ref[...] = up[(seq - 2) * nb:]
        return _conv3(up, ext, cw_ref, cb_ref, nb, 0)

    a = half(wua_ref, sta_ref, cwa_ref, cba_ref, csa_ref)
    g = half(wug_ref, stg_ref, cwg_ref, cbg_ref, csg_ref)
    _ffn_tail(j, acc_scr, _gelu(a) * g, wd_ref)

    @pl.when(j == pl.num_programs(0) - 1)
    def _():
        for l in range(seq):
            sl = slice(l * nb, (l + 1) * nb)
            y_ref[sl, :] = _rms(x1_ref[sl, :] + g2_ref[...] * acc_scr[sl, :]) * fn_ref[...]


def _ffn_sample(h2, x1, mod, final_norm, st, w_up_bf, conv_w, conv_b, w_down_bf, *, nb, seq, tf=512):
    r, d = x1.shape
    dff = w_down_bf.shape[0]
    nf = dff // tf
    kern = functools.partial(_ffn_sample_kernel, nb=nb, seq=seq)
    return pl.pallas_call(
        kern,
        grid=(nf,),
        in_specs=[pl.BlockSpec((r, d), lambda j: (0, 0)),
                  pl.BlockSpec((r, d), lambda j: (0, 0)),
                  pl.BlockSpec((nb, d), lambda j: (0, 5)),
                  pl.BlockSpec((1, d), lambda j: (0, 0)),
                  pl.BlockSpec((2 * nb, tf), lambda j: (0, j)),
                  pl.BlockSpec((2 * nb, tf), lambda j: (0, nf + j)),
                  pl.BlockSpec((d, tf), lambda j: (0, j)),
                  pl.BlockSpec((d, tf), lambda j: (0, nf + j)),
                  pl.BlockSpec((3, tf), lambda j: (0, j)),
                  pl.BlockSpec((3, tf), lambda j: (0, nf + j)),
                  pl.BlockSpec((1, tf), lambda j: (0, j)),
                  pl.BlockSpec((1, tf), lambda j: (0, nf + j)),
                  pl.BlockSpec((tf, d), lambda j: (j, 0))],
        out_specs=[pl.BlockSpec((r, d), lambda j: (0, 0)),
                   pl.BlockSpec((2 * nb, tf), lambda j: (0, j)),
                   pl.BlockSpec((2 * nb, tf), lambda j: (0, j))],
        out_shape=[jax.ShapeDtypeStruct((r, d), F32), jax.ShapeDtypeStruct((2 * nb, dff), F32),
                   jax.ShapeDtypeStruct((2 * nb, dff), F32)],
        scratch_shapes=[pltpu.VMEM((r, d), F32)],
        compiler_params=_cparams("arbitrary"),
        name="ffn_sample",
    )(h2, x1, mod, final_norm, st, st, w_up_bf, w_up_bf, conv_w, conv_w, conv_b, conv_b, w_down_bf)


def _layer(xp, xs, cp, cs, st_gla, st_re, st_im, st_conv, w, final_norm):
    b, l, d = xp.shape
    nb, seq, _ = xs.shape
    _, h, dk, dv = st_gla.shape
    kw, gw = h * dk, h * dv
    g, p, ch = w['s5_b_re'].shape
    sw = g * ch
    rank = w['w_a2'].shape[0]
    dff = w['w_down'].shape[0]
    main_w = 2 * kw + 2 * gw
    row = lambda t: t.reshape(1, -1)

    w_in_bf = w['w_in'].astype(BF16)
    w_u_bf = w['w_in'][:, main_w + rank:].astype(BF16)
    w_a_bf = w['w_in'][:, main_w:main_w + rank].astype(BF16)
    w_glu_bf = w['w_glu'].astype(BF16)
    w_out_bf = w['w_out'].astype(BF16)
    w_up_bf = w['w_up'].astype(BF16)
    w_down_bf = w['w_down'].astype(BF16)

    n_c = b + nb
    n_c_pad = -(-n_c // 8) * 8
    c_all = jnp.concatenate([cp, cs, jnp.zeros((n_c_pad - n_c, d), F32)], axis=0)
    mod = _ada(c_all, w['w_ada'], row(w['b_ada']))
    mod_p = mod[:b].reshape(b, 1, 6 * d)
    mod_s = mod[b:n_c]

    xp2 = xp.reshape(b * l, d)
    xs2 = jnp.swapaxes(xs, 0, 1).reshape(seq * nb, d)
    tm_p = min(TM_PROMPT, l)
    tps = l // tm_p
    seq_of = lambda i: i // tps

    proj = functools.partial(_inproj, norm1=row(w['norm1']), w_in_bf=w_in_bf, w_u_bf=w_u_bf, w_a_bf=w_a_bf,
                             w_a2=w['w_a2'], b_a2=row(w['b_a2']), main_w=main_w)
    qkvg_p, u_p, la_p = proj(xp2, mod_p, seq_of, tm=tm_p, n_slabs=1)
    qkvg_s, u_s, la_s = proj(xs2, mod_s, None, tm=seq * nb, n_slabs=seq)

    gn = row(w['gla_norm'])
    o_p, gla_p = _gla_prompt(qkvg_p, la_p, gn, b=b, l=l, h=h, dk=dk, dv=dv)
    o_s, gla_s = _gla_sample(qkvg_s, la_s, gn, st_gla, nb=nb, seq=seq, h=h, dk=dk, dv=dv)

    s5w = (w['s5_lam_re'], w['s5_lam_im'], w['s5_log_step'], w['s5_b_re'], w['s5_b_im'],
           w['s5_c_re'], w['s5_c_im'], w['s5_d'])
    cp_ = math.gcd(l, S5_CHUNK)
    n_chunk = l // cp_
    m_p, wst_p, v_p, lc_p = _s5_prep(*s5w, c_mat=cp_, c_real=cp_)
    y5_p, re_p, im_p = _s5_prompt(u_p, m_p, wst_p, v_p, lc_p, n_seq=b, n_chunk=n_chunk, c=cp_, ch=ch)
    re_p, im_p = re_p.reshape(b, g, p), im_p.reshape(b, g, p)

    m_s, wst_s, v_s, lc_s = _s5_prep(*s5w, c_mat=S5_CHUNK_SAMPLE, c_real=seq)
    y5_s, re_s, im_s = _s5_sample(u_s, st_re.reshape(nb, g * p), st_im.reshape(nb, g * p), m_s, wst_s, v_s, lc_s,
                                  seq=seq, nb=nb, ch=ch)
    re_s, im_s = re_s.reshape(nb, g, p), im_s.reshape(nb, g, p)

    mixer = functools.partial(_mix, norm2=row(w['norm2']), w_glu_bf=w_glu_bf, b_glu=row(w['b_glu']),
                              w_out_bf=w_out_bf)
    tm_mix = min(TM_MIX, l)
    x1_p, h2_p = mixer(o_p, y5_p, xp2, mod_p, lambda i: i // (l // tm_mix), tm=tm_mix, n_slabs=1)
    x1_s, h2_s = mixer(o_s, y5_s, xs2, mod_s, None, tm=seq * nb, n_slabs=seq)

    fn = row(final_norm)
    conv_w, conv_b = w['conv_w'], row(w['conv_b'])
    yp, cs_p = _ffn_prompt(h2_p, x1_p, mod_p, fn, w_up_bf, conv_w, conv_b, w_down_bf, b=b, l=l, tm=tm_p)
    conv_p = cs_p[tps - 1::tps].reshape(b, 2, 2 * dff)
    st_slab = jnp.swapaxes(st_conv, 0, 1).reshape(2 * nb, 2 * dff)
    ys, csa, csg = _ffn_sample(h2_s, x1_s, mod_s, fn, st_slab, w_up_bf, conv_w, conv_b, w_down_bf,
                               nb=nb, seq=seq)
    conv_s = jnp.swapaxes(jnp.concatenate([csa, csg], axis=-1).reshape(2, nb, 2 * dff), 0, 1)

    yp = yp.reshape(b, l, d)
    ys = jnp.swapaxes(ys.reshape(seq, nb, d), 0, 1)
    return yp, ys, (gla_p, re_p, im_p, conv_p), (gla_s, re_s, im_s, conv_s)


def kernel(x_prompt, x_sample, c_prompt, c_sample, state_gla, state_s5_re, state_s5_im, state_conv, w_ada, b_ada, norm1, w_in, w_a2, b_a2, gla_norm, s5_lam_re, s5_lam_im, s5_log_step, s5_b_re, s5_b_im, s5_c_re, s5_c_im, s5_d, w_glu, b_glu, w_out, norm2, w_up, conv_w, conv_b, w_down, final_norm):
    depth = w_ada.shape[0]
    assert depth == 1, "the final norm is fused into the last layer's FFN; only depth 1 is wired up"
    w = dict(w_ada=w_ada[0], b_ada=b_ada[0], norm1=norm1[0], w_in=w_in[0], w_a2=w_a2[0], b_a2=b_a2[0],
             gla_norm=gla_norm[0], s5_lam_re=s5_lam_re[0], s5_lam_im=s5_lam_im[0], s5_log_step=s5_log_step[0],
             s5_b_re=s5_b_re[0], s5_b_im=s5_b_im[0], s5_c_re=s5_c_re[0], s5_c_im=s5_c_im[0], s5_d=s5_d[0],
             w_glu=w_glu[0], b_glu=b_glu[0], w_out=w_out[0], norm2=norm2[0], w_up=w_up[0], conv_w=conv_w[0],
             conv_b=conv_b[0], w_down=w_down[0])
    yp, ys, sp, ss = _layer(x_prompt, x_sample, c_prompt, c_sample, state_gla[0], state_s5_re[0],
                            state_s5_im[0], state_conv[0], w, final_norm)
    stack = lambda t: t[None]
    return (yp, ys, stack(sp[0]), stack(sp[1]), stack(sp[2]), stack(sp[3]),
            stack(ss[0]), stack(ss[1]), stack(ss[2]), stack(ss[3]))
```

```python
import functools
import math

import jax
import jax.numpy as jnp
from jax import lax
from jax.experimental import pallas as pl
from jax.experimental.pallas import tpu as pltpu

F32 = jnp.float32
BF16 = jnp.bfloat16

NORM_EPS = 1e-6
GLA_TAU = 16.0
GLA_CHUNK = 64
GLA_SUB = 8
LOG2_E = 1.4426950408889634
GLA_SEQ_BLOCK = 512
LANES = 128
GLA_PAD = 128
S5_CHUNK = 16
S5_CHUNK_SAMPLE = 8
VMEM_LIMIT_BYTES = 56 * 1024 * 1024
TM_PROMPT = 512
TM_MIX = 512


def _cparams(*sem):
    return pltpu.CompilerParams(dimension_semantics=sem, vmem_limit_bytes=VMEM_LIMIT_BYTES)


def _dot(a, b):
    return jnp.dot(a.astype(BF16), b.astype(BF16), preferred_element_type=F32)


def _dot_nt(a, b):
    return lax.dot_general(a.astype(BF16), b.astype(BF16), (((1,), (1,)), ((), ())),
                           preferred_element_type=F32)


def _dot_f32(a, b):
    return jnp.dot(a, b, preferred_element_type=F32, precision=lax.Precision.HIGHEST)


def _rms(x):
    return x * lax.rsqrt(jnp.mean(x * x, axis=-1, keepdims=True) + NORM_EPS)


def _gelu(x):
    return 0.5 * x * (1.0 + jnp.tanh(math.sqrt(2.0 / math.pi) * (x + 0.044715 * (x * x * x))))


def _sigmoid(x):
    return 1.0 / (1.0 + jnp.exp(-x))


def _ada_kernel(c_ref, w_ref, b_ref, o_ref):
    c = c_ref[...]
    o_ref[...] = _dot(c * _sigmoid(c), w_ref[...]) + b_ref[...]


def _ada(c_all, w_ada, b_ada, tn=1024):
    m, d = c_all.shape
    n = w_ada.shape[1]
    return pl.pallas_call(
        _ada_kernel,
        grid=(n // tn,),
        in_specs=[pl.BlockSpec((m, d), lambda j: (0, 0)),
                  pl.BlockSpec((d, tn), lambda j: (0, j)),
                  pl.BlockSpec((1, tn), lambda j: (0, j))],
        out_specs=pl.BlockSpec((m, tn), lambda j: (0, j)),
        out_shape=jax.ShapeDtypeStruct((m, n), F32),
        compiler_params=_cparams("arbitrary"),
        name="ada",
    )(c_all, w_ada, b_ada)


def _resident(shape, index_map):
    return pl.BlockSpec(shape, index_map, pipeline_mode=pl.Buffered(1))


def _inproj_kernel(x_ref, sh_ref, sc_ref, n1_ref, wm_ref, wu_ref, wa_ref, wa2_ref, ba2_ref,
                   qkvg_ref, u_ref, la_ref, h_scr, *, n_slabs, tn):
    rows = x_ref.shape[0] // n_slabs
    for l in range(n_slabs):
        sl = slice(l * rows, (l + 1) * rows)
        h = _rms(x_ref[sl, :]) * n1_ref[...] * (1.0 + sc_ref[...]) + sh_ref[...]
        h_scr[sl, :] = h.astype(BF16)
    a_lr = jnp.dot(h_scr[...], wa_ref[...], preferred_element_type=F32)
    z = _dot(a_lr, wa2_ref[...]) + ba2_ref[...]
    la_ref[...] = (jnp.minimum(z, 0.0) - jnp.log(1.0 + jnp.exp(-jnp.abs(z)))) / GLA_TAU
    for out_ref, w_ref in ((qkvg_ref, wm_ref), (u_ref, wu_ref)):
        for c0 in range(0, out_ref.shape[1], tn):
            part = jnp.dot(h_scr[...], w_ref[:, c0:c0 + tn], preferred_element_type=F32)
            out_ref[:, c0:c0 + tn] = part.astype(out_ref.dtype)


def _inproj(x, mod, mod_row, norm1, w_in_bf, w_u_bf, w_a_bf, w_a2, b_a2, *, tm, n_slabs, main_w, main_dtype,
            tn=512):
    r, d = x.shape
    sw = w_u_bf.shape[1]
    kw = w_a2.shape[1]
    rank = w_a_bf.shape[1]
    if mod.ndim == 3:
        mspec = lambda k: pl.BlockSpec((None, 1, d), lambda i: (mod_row(i), 0, k))
    else:
        mspec = lambda k: pl.BlockSpec((mod.shape[0], d), lambda i: (0, k))
    kern = functools.partial(_inproj_kernel, n_slabs=n_slabs, tn=tn)
    return pl.pallas_call(
        kern,
        grid=(r // tm,),
        in_specs=[pl.BlockSpec((tm, d), lambda i: (i, 0)),
                  mspec(0), mspec(1),
                  pl.BlockSpec((1, d), lambda i: (0, 0)),
                  _resident((d, main_w), lambda i: (0, 0)),
                  _resident((d, sw), lambda i: (0, 0)),
                  _resident((d, rank), lambda i: (0, 0)),
                  pl.BlockSpec((rank, kw), lambda i: (0, 0)),
                  pl.BlockSpec((1, kw), lambda i: (0, 0))],
        out_specs=[pl.BlockSpec((tm, main_w), lambda i: (i, 0)),
                   pl.BlockSpec((tm, sw), lambda i: (i, 0)),
                   pl.BlockSpec((tm, kw), lambda i: (i, 0))],
        out_shape=[jax.ShapeDtypeStruct((r, main_w), main_dtype),
                   jax.ShapeDtypeStruct((r, sw), F32),
                   jax.ShapeDtypeStruct((r, kw), F32)],
        scratch_shapes=[pltpu.VMEM((tm, d), BF16)],
        compiler_params=_cparams("arbitrary"),
        name="inproj",
    )(x, mod, mod, norm1, w_in_bf, w_u_bf, w_a_bf, w_a2, b_a2)


def _gla_chunk(q, k, v, ga, states, *, h, sub):
    c = q.shape[0]
    dk, dv = q.shape[1] // h, v.shape[1] // h
    rows = lax.broadcasted_iota(jnp.int32, (c, c), 0)
    cols = lax.broadcasted_iota(jnp.int32, (c, c), 1)
    sums, widths = [rows >= cols], []
    w = sub
    while w < c:
        sums.append(cols <= (rows // (2 * w)) * (2 * w) + (w - 1))
        widths.append(w)
        w *= 2
    sums.append(cols >= 0)
    pmat = jnp.concatenate(sums, axis=0).astype(F32).astype(BF16)
    ga_hi = ga.astype(BF16)
    ga_lo = (ga - ga_hi.astype(F32)).astype(BF16)
    tot = (jnp.dot(pmat, ga_hi, preferred_element_type=F32)
           + jnp.dot(pmat, ga_lo, preferred_element_type=F32))
    tot = tot * LOG2_E
    cum = tot[:c]
    last = tot[(len(sums) - 1) * c:]

    row_w = lax.broadcasted_iota(jnp.int32, q.shape, 0)
    lane_o = lax.broadcasted_iota(jnp.int32, (c, LANES), 1)
    by_off = [jnp.zeros((c, LANES), F32) for _ in range(h)]
    for off in range(sub):
        k_sh = k if off == 0 else pltpu.roll(k, off, axis=0)
        c_sh = cum if off == 0 else pltpu.roll(cum, off, axis=0)
        prod = q * k_sh * jnp.exp2(jnp.minimum(cum - c_sh, 0.0))
        for hh in range(h):
            col = jnp.sum(prod[:, hh * dk:(hh + 1) * dk], axis=-1, keepdims=True)
            by_off[hh] = jnp.where(lane_o == c - 1 - off, col, by_off[hh])
    same_sub = rows // sub == cols // sub
    att = []
    for hh in range(h):
        moved = pltpu.roll(by_off[hh], LANES - (c - 1), axis=1, stride=1, stride_axis=0)
        att.append(jnp.where(same_sub, moved[:, :c], 0.0))
    for lvl, w in enumerate(widths):
        ref = tot[(lvl + 1) * c:(lvl + 2) * c]
        odd = (row_w // w) % 2 == 1
        x = jnp.where(odd, q, k) * jnp.exp2(jnp.where(odd, cum - ref, ref - cum))
        pair = (rows // (2 * w) == cols // (2 * w)) & ((rows // w) % 2 == 1) & ((cols // w) % 2 == 0)
        for hh in range(h):
            xh = x[:, hh * dk:(hh + 1) * dk]
            att[hh] = att[hh] + jnp.where(pair, _dot_nt(xh, xh), 0.0)

    q_in = q * jnp.exp2(cum)
    k_out = k * jnp.exp2(last - cum)
    e_last = jnp.exp2(last[0:1])
    tail_row = lax.broadcasted_iota(jnp.int32, (GLA_PAD - c, dk), 0)
    outs, new_states = [], []
    for hh in range(h):
        ck, cv = slice(hh * dk, (hh + 1) * dk), slice(hh * dv, (hh + 1) * dv)
        outs.append(_dot(att[hh], v[:, cv]) + _dot(q_in[:, ck], states[hh]))
        tail = jnp.where(tail_row == 0, e_last[:, ck], 0.0)
        kt = jnp.concatenate([k_out[:, ck], tail], axis=0).T
        new_states.append(states[hh] * kt[:, c:c + 1] + _dot(kt[:, :c], v[:, cv]))
    return outs, new_states


def _gla_finish(o, g, gn):
    return _rms(o) * gn * (g * _sigmoid(g))


def _gla_prompt_kernel(q_ref, k_ref, v_ref, g_ref, la_ref, gn_ref, o_ref, s_ref, s_scr, *, h, dk, dv, chunk, sub,
                       scale):
    blk = pl.program_id(1)

    @pl.when(blk == 0)
    def _():
        s_scr[...] = jnp.zeros_like(s_scr)

    def body(n, carry):
        r = pl.ds(pl.multiple_of(n * chunk, chunk), chunk)
        outs, new_states = _gla_chunk(q_ref[r, :].astype(F32) * scale, k_ref[r, :].astype(F32),
                                      v_ref[r, :].astype(F32), la_ref[r, :],
                                      [s_scr[hh] for hh in range(h)], h=h, sub=sub)
        for hh in range(h):
            cv = slice(hh * dv, (hh + 1) * dv)
            s_scr[hh] = new_states[hh]
            o_ref[r, cv] = _gla_finish(outs[hh], g_ref[r, cv].astype(F32), gn_ref[...]).astype(o_ref.dtype)
        return carry

    lax.fori_loop(0, q_ref.shape[0] // chunk, body, 0)

    @pl.when(blk == pl.num_programs(1) - 1)
    def _():
        s_ref[...] = s_scr[...]


def _gla_prompt(qkvg, la, gla_norm, *, b, l, h, dk, dv):
    kw, gw = h * dk, h * dv
    chunk = math.gcd(l, GLA_CHUNK)
    sub = math.gcd(chunk, GLA_SUB)
    lb = math.gcd(l, GLA_SEQ_BLOCK)
    nlb = l // lb
    kern = functools.partial(_gla_prompt_kernel, h=h, dk=dk, dv=dv, chunk=chunk, sub=sub, scale=dk ** -0.5)
    return pl.pallas_call(
        kern,
        grid=(b, nlb),
        in_specs=[pl.BlockSpec((lb, kw), lambda i, j: (i * nlb + j, 0)),
                  pl.BlockSpec((lb, kw), lambda i, j: (i * nlb + j, 1)),
                  pl.BlockSpec((lb, gw), lambda i, j: (i * nlb + j, 2 * kw // gw)),
                  pl.BlockSpec((lb, gw), lambda i, j: (i * nlb + j, 2 * kw // gw + 1)),
                  pl.BlockSpec((lb, kw), lambda i, j: (i * nlb + j, 0)),
                  pl.BlockSpec((1, dv), lambda i, j: (0, 0))],
        out_specs=[pl.BlockSpec((lb, gw), lambda i, j: (i * nlb + j, 0)),
                   pl.BlockSpec((None, h, dk, dv), lambda i, j: (i, 0, 0, 0))],
        out_shape=[jax.ShapeDtypeStruct((b * l, gw), BF16),
                   jax.ShapeDtypeStruct((b, h, dk, dv), F32)],
        scratch_shapes=[pltpu.VMEM((h, dk, dv), F32)],
        compiler_params=_cparams("arbitrary", "arbitrary"),
        name="gla_prompt",
    )(qkvg, qkvg, qkvg, qkvg, la, gla_norm)


def _gla_sample_kernel(qkvg_ref, la_ref, gn_ref, s0_ref, o_ref, s_ref, *, nb, seq, h, dk, dv, pad, scale):
    bt = s0_ref.shape[0]
    kw, gw = h * dk, h * dv
    in_w = 2 * kw + 2 * gw
    i = pl.program_id(0)

    def rows_of(b, width, col):
        per_row = width // LANES
        return pl.ds(b * per_row + col // LANES, seq, stride=nb * per_row)

    def gather(ref, b, width, col, n_col):
        parts = [ref[rows_of(b, width, col + c), :] for c in range(0, n_col, LANES)]
        x = parts[0] if len(parts) == 1 else jnp.concatenate(parts, axis=1)
        return jnp.concatenate([x, jnp.zeros((pad - seq, n_col), F32)], axis=0)

    def body(bb, carry):
        b = i * bt + bb
        q = gather(qkvg_ref, b, in_w, 0, kw) * scale
        k = gather(qkvg_ref, b, in_w, kw, kw)
        v = gather(qkvg_ref, b, in_w, 2 * kw, gw)
        g = gather(qkvg_ref, b, in_w, 2 * kw + gw, gw)
        ga = gather(la_ref, b, kw, 0, kw)
        outs, new_states = _gla_chunk(q, k, v, ga, [s0_ref[bb, hh] for hh in range(h)], h=h, sub=pad)
        for hh in range(h):
            s_ref[bb, hh] = new_states[hh]
            res = _gla_finish(outs[hh], g[:, hh * dv:(hh + 1) * dv], gn_ref[...])
            for c in range(0, dv, LANES):
                o_ref[rows_of(b, gw, hh * dv + c), :] = res[:seq, c:c + LANES]
        return carry

    lax.fori_loop(0, bt, body, 0, unroll=4)


def _gla_sample(qkvg, la, gla_norm, s0, *, nb, seq, h, dk, dv, bt=8):
    gw = h * dv
    pad = 8
    assert dk % LANES == 0 and dv % LANES == 0
    qkvg, la = qkvg.reshape(-1, LANES), la.reshape(-1, LANES)
    o_rows = nb * seq * gw // LANES
    kern = functools.partial(_gla_sample_kernel, nb=nb, seq=seq, h=h, dk=dk, dv=dv, pad=pad, scale=dk ** -0.5)
    o, s_new = pl.pallas_call(
        kern,
        grid=(nb // bt,),
        in_specs=[pl.BlockSpec(qkvg.shape, lambda i: (0, 0)),
                  pl.BlockSpec(la.shape, lambda i: (0, 0)),
                  pl.BlockSpec((1, dv), lambda i: (0, 0)),
                  pl.BlockSpec((bt, h, dk, dv), lambda i: (i, 0, 0, 0))],
        out_specs=[pl.BlockSpec((o_rows, LANES), lambda i: (0, 0)),
                   pl.BlockSpec((bt, h, dk, dv), lambda i: (i, 0, 0, 0))],
        out_shape=[jax.ShapeDtypeStruct((o_rows, LANES), F32),
                   jax.ShapeDtypeStruct((nb, h, dk, dv), F32)],
        compiler_params=_cparams("arbitrary"),
        name="gla_sample",
    )(qkvg, la, gla_norm, s0)
    return o.reshape(nb * seq, gw), s_new


def _s5_prep_kernel(lamc_ref, lamr_ref, ls_ref, bt_re_ref, bt_im_ref, btile_re_ref, btile_im_ref,
                    ctile_re_ref, ctile_im_ref, d_ref, m_ref, wst_ref, v_ref, lc_ref, *, ch, c_mat, c_real):
    w = ch * c_mat
    p = lamc_ref.shape[1]
    lane_m = lax.broadcasted_iota(jnp.int32, (p, LANES), 1)
    m_f = jnp.where(lane_m <= c_mat, lane_m, 0).astype(F32)
    sel = lax.broadcasted_iota(jnp.int32, (LANES, w), 0)
    tau = lax.broadcasted_iota(jnp.int32, (LANES, w), 1) // ch
    spread_pw = (sel == tau).astype(F32)
    spread_pv = (sel == tau + 1).astype(F32)
    spread_ps = (sel == jnp.maximum(c_real - 1 - tau, 0)).astype(F32)
    rr = lax.broadcasted_iota(jnp.int32, (ch, w), 0)
    cc = lax.broadcasted_iota(jnp.int32, (ch, w), 1)

    def bbar_coef(lam_re, lam_im, lb_re, lb_im):
        den = lam_re * lam_re + lam_im * lam_im
        x, y = lb_re - 1.0, lb_im
        return (x * lam_re + y * lam_im) / den, (y * lam_re - x * lam_im) / den

    for gi in range(m_ref.shape[0]):
        dt = jnp.exp(ls_ref[gi])
        lam_re, lam_im = lamc_ref[gi, :, 0:1], lamc_ref[gi, :, 1:2]
        mag = jnp.exp(lam_re * dt * m_f)
        t_re, t_im = mag * jnp.cos(lam_im * dt * m_f), mag * jnp.sin(lam_im * dt * m_f)
        cf_re, cf_im = bbar_coef(lam_re, lam_im, t_re[:, 1:2], t_im[:, 1:2])

        ct_re, ct_im = ctile_re_ref[gi], ctile_im_ref[gi]
        pw_re, pw_im = _dot_f32(t_re, spread_pw), _dot_f32(t_im, spread_pw)
        cl_re = ct_re * pw_re - ct_im * pw_im
        cl_im = ct_re * pw_im + ct_im * pw_re
        pv_re, pv_im = _dot_f32(t_re, spread_pv), _dot_f32(t_im, spread_pv)
        v_ref[gi, 0:p, :] = (ct_re * pv_re - ct_im * pv_im).astype(v_ref.dtype)
        v_ref[gi, p:2 * p, :] = (-(ct_re * pv_im + ct_im * pv_re)).astype(v_ref.dtype)

        bb_re = cf_re * btile_re_ref[gi] - cf_im * btile_im_ref[gi]
        bb_im = cf_re * btile_im_ref[gi] + cf_im * btile_re_ref[gi]
        ps_re, ps_im = _dot_f32(t_re, spread_ps), _dot_f32(t_im, spread_ps)
        wst_ref[gi, 0:p, :] = (bb_re * ps_re - bb_im * ps_im).astype(wst_ref.dtype)
        wst_ref[gi, p:2 * p, :] = (bb_re * ps_im + bb_im * ps_re).astype(wst_ref.dtype)

        lr_re, lr_im = lamr_ref[gi, 0:1, :], lamr_ref[gi, 1:2, :]
        ar, tr = lr_re * dt, lr_im * dt
        ea = jnp.exp(ar)
        rf_re, rf_im = bbar_coef(lr_re, lr_im, ea * jnp.cos(tr), ea * jnp.sin(tr))
        bbt_re = rf_re * bt_re_ref[gi] - rf_im * bt_im_ref[gi]
        bbt_im = rf_re * bt_im_ref[gi] + rf_im * bt_re_ref[gi]
        kcat = _dot_f32(bbt_re, cl_re) - _dot_f32(bbt_im, cl_im)
        kcat = kcat + jnp.where(rr == cc, d_ref[gi], 0.0)
        for i in range(c_mat):
            shifted = kcat if i == 0 else pltpu.roll(kcat, ch * i, axis=1)
            m_ref[gi, i * ch:(i + 1) * ch, :] = jnp.where(cc >= ch * i, shifted, 0.0).astype(m_ref.dtype)

        ec = jnp.exp(ar * float(c_real))
        lc_ref[gi, 0:1, :] = ec * jnp.cos(tr * float(c_real))
        lc_ref[gi, 1:2, :] = ec * jnp.sin(tr * float(c_real))


def _s5_prep(lam_re, lam_im, log_step, b_re, b_im, c_re, c_im, d_skip, *, c_mat, c_real):
    g, p, ch = b_re.shape
    w = ch * c_mat
    gb = LANES // ch
    lam_col = jnp.stack([lam_re, lam_im], axis=-1)
    lam_row = jnp.stack([lam_re, lam_im], axis=1)
    ls = log_step.reshape(g, 1, 1)
    tile = lambda t: jnp.tile(t, (1, 1, c_mat))
    bt = lambda t: jnp.swapaxes(t, 1, 2)
    d_row = jnp.pad(d_skip.reshape(g, 1, ch), ((0, 0), (0, 0), (0, w - ch)))
    spec3 = lambda s: pl.BlockSpec((gb,) + s, lambda i: (i, 0, 0))
    kern = functools.partial(_s5_prep_kernel, ch=ch, c_mat=c_mat, c_real=c_real)
    return pl.pallas_call(
        kern,
        grid=(g // gb,),
        in_specs=[spec3((p, 2)), spec3((2, p)), spec3((1, 1)), spec3((ch, p)), spec3((ch, p)),
                  spec3((p, w)), spec3((p, w)), spec3((p, w)), spec3((p, w)), spec3((1, w))],
        out_specs=[spec3((w, w)), spec3((2 * p, w)), spec3((2 * p, w)), spec3((2, p))],
        out_shape=[jax.ShapeDtypeStruct((g, w, w), BF16), jax.ShapeDtypeStruct((g, 2 * p, w), BF16),
                   jax.ShapeDtypeStruct((g, 2 * p, w), BF16), jax.ShapeDtypeStruct((g, 2, p), F32)],
        compiler_params=_cparams("arbitrary"),
        name="s5_prep",
    )(lam_col, lam_row, ls, bt(b_re), bt(b_im), tile(b_re), tile(b_im),
      tile(jnp.swapaxes(c_re, 1, 2)), tile(jnp.swapaxes(c_im, 1, 2)), d_row)


def _cmul_rows(z, lr, li):
    p = lr.shape[1]
    coef_a = jnp.concatenate([lr, lr], axis=1)
    coef_b = jnp.concatenate([-li, li], axis=1)
    return z * coef_a + pltpu.roll(z, p, axis=1) * coef_b


def _regroup(parts, gi, ch):
    return jnp.concatenate([t[:, gi * ch:(gi + 1) * ch] for t in parts], axis=1)


def _s5_prompt_kernel(u_ref, m_ref, wst_ref, v_ref, lc_ref, y_ref, hre_ref, him_ref, *, ch, c, n_seq, n_chunk):
    gb = m_ref.shape[0]
    r = n_seq * n_chunk
    p = lc_ref.shape[2]
    toks = [u_ref[pl.ds(i, r, stride=c), :] for i in range(c)]
    pos = lax.broadcasted_iota(jnp.int32, (r, 2 * p), 0) % n_chunk
    ys, h_re, h_im = [], [], []
    for gi in range(gb):
        x = _regroup(toks, gi, ch).astype(BF16)
        z = lax.dot_general(x, wst_ref[gi], (((1,), (1,)), ((), ())), preferred_element_type=F32)
        lr, li = lc_ref[gi, 0:1, :], lc_ref[gi, 1:2, :]
        d = 1
        while d < n_chunk:
            zs = jnp.where(pos >= d, pltpu.roll(z, d, axis=0), 0.0)
            z = z + _cmul_rows(zs, lr, li)
            lr, li = lr * lr - li * li, 2.0 * lr * li
            d *= 2
        h_in = jnp.where(pos >= 1, pltpu.roll(z, 1, axis=0), 0.0)
        ys.append(jnp.dot(x, m_ref[gi], preferred_element_type=F32)
                  + jnp.dot(h_in.astype(BF16), v_ref[gi], preferred_element_type=F32))
        hf = jnp.concatenate([z[(s + 1) * n_chunk - 1:(s + 1) * n_chunk] for s in range(n_seq)], axis=0)
        h_re.append(hf[:, :p])
        h_im.append(hf[:, p:])
    for j in range(c):
        y_ref[pl.ds(j, r, stride=c), :] = _regroup(ys, j, ch)
    hre_ref[...] = jnp.concatenate(h_re, axis=1)
    him_ref[...] = jnp.concatenate(h_im, axis=1)


def _s5_prompt(u, m, wst, v, lc, *, n_seq, n_chunk, c, ch):
    t, sw = u.shape
    g, w, _ = m.shape
    p2 = wst.shape[1]
    p = p2 // 2
    gb = LANES // ch
    spec3 = lambda s: pl.BlockSpec((gb,) + s, lambda i: (i, 0, 0))
    kern = functools.partial(_s5_prompt_kernel, ch=ch, c=c, n_seq=n_seq, n_chunk=n_chunk)
    return pl.pallas_call(
        kern,
        grid=(g // gb,),
        in_specs=[pl.BlockSpec((t, LANES), lambda i: (0, i)),
                  spec3((w, w)), spec3((p2, w)), spec3((p2, w)), spec3((2, p))],
        out_specs=[pl.BlockSpec((t, LANES), lambda i: (0, i)),
                   pl.BlockSpec((n_seq, gb * p), lambda i: (0, i)),
                   pl.BlockSpec((n_seq, gb * p), lambda i: (0, i))],
        out_shape=[jax.ShapeDtypeStruct((t, sw), F32), jax.ShapeDtypeStruct((n_seq, g * p), F32),
                   jax.ShapeDtypeStruct((n_seq, g * p), F32)],
        compiler_params=_cparams("arbitrary"),
        name="s5_prompt",
    )(u, m, wst, v, lc)


def _s5_sample_kernel(u_ref, sre_ref, sim_ref, m_ref, wst_ref, v_ref, lc_ref, y_ref, hre_ref, him_ref, *,
                      ch, seq, nb):
    gb = m_ref.shape[0]
    p = lc_ref.shape[2]
    w = m_ref.shape[1]
    toks = [u_ref[l * nb:(l + 1) * nb, :] for l in range(seq)]
    ys, h_re, h_im = [], [], []
    for gi in range(gb):
        x = jnp.concatenate([_regroup(toks, gi, ch), jnp.zeros((nb, w - seq * ch), F32)], axis=1).astype(BF16)
        h0 = jnp.concatenate([sre_ref[:, gi * p:(gi + 1) * p], sim_ref[:, gi * p:(gi + 1) * p]], axis=1)
        ys.append(jnp.dot(x, m_ref[gi], preferred_element_type=F32)
                  + jnp.dot(h0.astype(BF16), v_ref[gi], preferred_element_type=F32))
        hf = (_cmul_rows(h0, lc_ref[gi, 0:1, :], lc_ref[gi, 1:2, :])
              + lax.dot_general(x, wst_ref[gi], (((1,), (1,)), ((), ())), preferred_element_type=F32))
        h_re.append(hf[:, :p])
        h_im.append(hf[:, p:])
    for l in range(seq):
        y_ref[l * nb:(l + 1) * nb, :] = _regroup(ys, l, ch)
    hre_ref[...] = jnp.concatenate(h_re, axis=1)
    him_ref[...] = jnp.concatenate(h_im, axis=1)


def _s5_sample(u, st_re, st_im, m, wst, v, lc, *, seq, nb, ch):
    t, sw = u.shape
    g, w, _ = m.shape
    p2 = wst.shape[1]
    p = p2 // 2
    gb = LANES // ch
    spec3 = lambda s: pl.BlockSpec((gb,) + s, lambda i: (i, 0, 0))
    sspec = pl.BlockSpec((nb, gb * p), lambda i: (0, i))
    kern = functools.partial(_s5_sample_kernel, ch=ch, seq=seq, nb=nb)
    return pl.pallas_call(
        kern,
        grid=(g // gb,),
        in_specs=[pl.BlockSpec((t, LANES), lambda i: (0, i)), sspec, sspec,
                  spec3((w, w)), spec3((p2, w)), spec3((p2, w)), spec3((2, p))],
        out_specs=[pl.BlockSpec((t, LANES), lambda i: (0, i)), sspec, sspec],
        out_shape=[jax.ShapeDtypeStruct((t, sw), F32), jax.ShapeDtypeStruct((nb, g * p), F32),
                   jax.ShapeDtypeStruct((nb, g * p), F32)],
        compiler_params=_cparams("arbitrary"),
        name="s5_sample",
    )(u, st_re, st_im, m, wst, v, lc)


def _mix_kernel(o_ref, y_ref, x_ref, g1_ref, sh2_ref, sc2_ref, n2_ref, wglu_ref, bglu_ref, wo_ref,
                x1_ref, h2_ref, *, n_slabs, tn):
    rows = x_ref.shape[0] // n_slabs
    gw = o_ref.shape[1]
    yg = _gelu(y_ref[...])
    z = (yg * _sigmoid(_dot(yg, wglu_ref[...]) + bglu_ref[...])).astype(BF16)
    ob = o_ref[...].astype(BF16)
    for c0 in range(0, x_ref.shape[1], tn):
        cols = slice(c0, c0 + tn)
        mix = (jnp.dot(ob, wo_ref[0:gw, cols], preferred_element_type=F32)
               + jnp.dot(z, wo_ref[gw:, cols], preferred_element_type=F32))
        for l in range(n_slabs):
            sl = slice(l * rows, (l + 1) * rows)
            x1_ref[sl, cols] = x_ref[sl, cols] + g1_ref[:, cols] * mix[sl, :]
    for l in range(n_slabs):
        sl = slice(l * rows, (l + 1) * rows)
        h2 = _rms(x1_ref[sl, :]) * n2_ref[...] * (1.0 + sc2_ref[...]) + sh2_ref[...]
        h2_ref[sl, :] = h2.astype(BF16)


def _mix(o, y, x, mod, mod_row, norm2, w_glu_bf, b_glu, w_out_bf, *, tm, n_slabs, tn=512):
    r, d = x.shape
    gw = o.shape[1]
    sw = y.shape[1]
    if mod.ndim == 3:
        mspec = lambda k: pl.BlockSpec((None, 1, d), lambda i: (mod_row(i), 0, k))
    else:
        mspec = lambda k: pl.BlockSpec((mod.shape[0], d), lambda i: (0, k))
    kern = functools.partial(_mix_kernel, n_slabs=n_slabs, tn=tn)
    return pl.pallas_call(
        kern,
        grid=(r // tm,),
        in_specs=[pl.BlockSpec((tm, gw), lambda i: (i, 0)),
                  pl.BlockSpec((tm, sw), lambda i: (i, 0)),
                  pl.BlockSpec((tm, d), lambda i: (i, 0)),
                  mspec(2), mspec(3), mspec(4),
                  pl.BlockSpec((1, d), lambda i: (0, 0)),
                  _resident((sw, sw), lambda i: (0, 0)),
                  pl.BlockSpec((1, sw), lambda i: (0, 0)),
                  _resident((gw + sw, d), lambda i: (0, 0))],
        out_specs=[pl.BlockSpec((tm, d), lambda i: (i, 0)),
                   pl.BlockSpec((tm, d), lambda i: (i, 0))],
        out_shape=[jax.ShapeDtypeStruct((r, d), F32), jax.ShapeDtypeStruct((r, d), BF16)],
        compiler_params=_cparams("arbitrary"),
        name="mix",
    )(o, y, x, mod, mod, mod, norm2, w_glu_bf, b_glu, w_out_bf)


def _conv3(up, ext, cw_ref, cb_ref, off1, off2):
    n = up.shape[0]
    return (cw_ref[2:3, :] * up + cw_ref[1:2, :] * ext[off1:off1 + n] + cw_ref[0:1, :] * ext[off2:off2 + n]
            + cb_ref[...])


def _ffn_tail(j, acc_scr, act, wd_ref):
    part = jnp.dot(act.astype(BF16), wd_ref[...], preferred_element_type=F32)

    @pl.when(j == 0)
    def _():
        acc_scr[...] = part

    @pl.when(j > 0)
    def _():
        acc_scr[...] += part


def _ffn_prompt_kernel(h_ref, halo_ref, x1_ref, g2_ref, fn_ref, wua_ref, wug_ref, cwa_ref, cwg_ref,
                       cba_ref, cbg_ref, wd_ref, y_ref, cs_ref, act_scr, x2_scr, *, tiles_per_seq, halo, nf, tf, tn):
    i, j = pl.program_id(0), pl.program_id(1)

    @pl.when(j < nf)
    def _():
        first = (i % tiles_per_seq) == 0
        h = h_ref[...]
        hh = halo_ref[...]
        tm = h.shape[0]

        def half(w_ref, cw_ref, cb_ref):
            up = jnp.dot(h, w_ref[...], preferred_element_type=F32)
            up_halo = jnp.where(first, 0.0, jnp.dot(hh, w_ref[...], preferred_element_type=F32))
            ext = jnp.concatenate([up_halo, up], axis=0)
            return _conv3(up, ext, cw_ref, cb_ref, halo - 1, halo - 2), up

        a, up_a = half(wua_ref, cwa_ref, cba_ref)
        g, up_g = half(wug_ref, cwg_ref, cbg_ref)
        for rr in range(2):
            row = tm - 2 + rr
            cs_ref[rr] = jnp.concatenate([up_a[row:row + 1], up_g[row:row + 1]], axis=0)
        act_scr[:, pl.ds(pl.multiple_of(j * tf, tf), tf)] = (_gelu(a) * g).astype(BF16)

    @pl.when(j >= nf)
    def _():
        cols = pl.ds(pl.multiple_of((j - nf) * tn, tn), tn)
        ff = jnp.dot(act_scr[...], wd_ref[...], preferred_element_type=F32)
        x2_scr[:, cols] = x1_ref[...] + g2_ref[:, cols] * ff

    @pl.when(j == pl.num_programs(1) - 1)
    def _():
        y_ref[...] = _rms(x2_scr[...]) * fn_ref[...]


def _ffn_prompt(h2, x1, mod, final_norm, w_up_bf, conv_w, conv_b, w_down_bf, *, b, l, tm, tf=512, tn=512, halo=16):
    r, d = x1.shape
    dff = w_down_bf.shape[0]
    nf = dff // tf
    nd = d // tn
    tps = l // tm
    up_j = lambda j: jnp.minimum(j, nf - 1)
    down_j = lambda j: jnp.maximum(j - nf, 0)
    kern = functools.partial(_ffn_prompt_kernel, tiles_per_seq=tps, halo=halo, nf=nf, tf=tf, tn=tn)
    return pl.pallas_call(
        kern,
        grid=(r // tm, nf + nd),
        in_specs=[pl.BlockSpec((tm, d), lambda i, j: (i, 0)),
                  pl.BlockSpec((halo, d), lambda i, j: (jnp.maximum(i * (tm // halo) - 1, 0), 0)),
                  pl.BlockSpec((tm, tn), lambda i, j: (i, down_j(j))),
                  pl.BlockSpec((None, 1, d), lambda i, j: (i // tps, 0, 5)),
                  pl.BlockSpec((1, d), lambda i, j: (0, 0)),
                  pl.BlockSpec((d, tf), lambda i, j: (0, up_j(j))),
                  pl.BlockSpec((d, tf), lambda i, j: (0, nf + up_j(j))),
                  pl.BlockSpec((3, tf), lambda i, j: (0, up_j(j))),
                  pl.BlockSpec((3, tf), lambda i, j: (0, nf + up_j(j))),
                  pl.BlockSpec((1, tf), lambda i, j: (0, up_j(j))),
                  pl.BlockSpec((1, tf), lambda i, j: (0, nf + up_j(j))),
                  pl.BlockSpec((dff, tn), lambda i, j: (0, down_j(j)))],
        out_specs=[pl.BlockSpec((tm, d), lambda i, j: (i, 0)),
                   pl.BlockSpec((None, 2, 2, tf), lambda i, j: (i, 0, 0, up_j(j)))],
        out_shape=[jax.ShapeDtypeStruct((r, d), F32), jax.ShapeDtypeStruct((r // tm, 2, 2, dff), F32)],
        scratch_shapes=[pltpu.VMEM((tm, dff), BF16), pltpu.VMEM((tm, d), F32)],
        compiler_params=_cparams("arbitrary", "arbitrary"),
        name="ffn_prompt",
    )(h2, h2, x1, mod, final_norm, w_up_bf, w_up_bf, conv_w, conv_w, conv_b, conv_b, w_down_bf)


def _ffn_sample_kernel(h_ref, x1_ref, g2_ref, fn_ref, sta_ref, stg_ref, wua_ref, wug_ref, cwa_ref, cwg_ref,
                       cba_ref, cbg_ref, wd_ref, y_ref, csa_ref, csg_ref, acc_scr, *, nb, seq):
    j = pl.program_id(0)
    h = h_ref[...]

    def half(w_ref, st_ref, cw_ref, cb_ref, cs_ref):
        up = jnp.dot(h, w_ref[...], preferred_element_type=F32)
        ext = jnp.concatenate([st_ref[...], up], axis=0)
        cs_ref[...] = up[(seq - 2) * nb:]
        return _conv3(up, ext, cw_ref, cb_ref, nb, 0)

    a = half(wua_ref, sta_ref, cwa_ref, cba_ref, csa_ref)
    g = half(wug_ref, stg_ref, cwg_ref, cbg_ref, csg_ref)
    _ffn_tail(j, acc_scr, _gelu(a) * g, wd_ref)

    @pl.when(j == pl.num_programs(0) - 1)
    def _():
        for l in range(seq):
            sl = slice(l * nb, (l + 1) * nb)
            y_ref[sl, :] = _rms(x1_ref[sl, :] + g2_ref[...] * acc_scr[sl, :]) * fn_ref[...]


def _ffn_sample(h2, x1, mod, final_norm, st, w_up_bf, conv_w, conv_b, w_down_bf, *, nb, seq, tf=512):
    r, d = x1.shape
    dff = w_down_bf.shape[0]
    nf = dff // tf
    kern = functools.partial(_ffn_sample_kernel, nb=nb, seq=seq)
    return pl.pallas_call(
        kern,
        grid=(nf,),
        in_specs=[pl.BlockSpec((r, d), lambda j: (0, 0)),
                  pl.BlockSpec((r, d), lambda j: (0, 0)),
                  pl.BlockSpec((nb, d), lambda j: (0, 5)),
                  pl.BlockSpec((1, d), lambda j: (0, 0)),
                  pl.BlockSpec((2 * nb, tf), lambda j: (0, j)),
                  pl.BlockSpec((2 * nb, tf), lambda j: (0, nf + j)),
                  pl.BlockSpec((d, tf), lambda j: (0, j)),
                  pl.BlockSpec((d, tf), lambda j: (0, nf + j)),
                  pl.BlockSpec((3, tf), lambda j: (0, j)),
                  pl.BlockSpec((3, tf), lambda j: (0, nf + j)),
                  pl.BlockSpec((1, tf), lambda j: (0, j)),
                  pl.BlockSpec((1, tf), lambda j: (0, nf + j)),
                  pl.BlockSpec((tf, d), lambda j: (j, 0))],
        out_specs=[pl.BlockSpec((r, d), lambda j: (0, 0)),
                   pl.BlockSpec((2 * nb, tf), lambda j: (0, j)),
                   pl.BlockSpec((2 * nb, tf), lambda j: (0, j))],
        out_shape=[jax.ShapeDtypeStruct((r, d), F32), jax.ShapeDtypeStruct((2 * nb, dff), F32),
                   jax.ShapeDtypeStruct((2 * nb, dff), F32)],
        scratch_shapes=[pltpu.VMEM((r, d), F32)],
        compiler_params=_cparams("arbitrary"),
        name="ffn_sample",
    )(h2, x1, mod, final_norm, st, st, w_up_bf, w_up_bf, conv_w, conv_w, conv_b, conv_b, w_down_bf)


def _layer(xp, xs, cp, cs, st_gla, st_re, st_im, st_conv, w, final_norm):
    b, l, d = xp.shape
    nb, seq, _ = xs.shape
    _, h, dk, dv = st_gla.shape
    kw, gw = h * dk, h * dv
    g, p, ch = w['s5_b_re'].shape
    sw = g * ch
    rank = w['w_a2'].shape[0]
    dff = w['w_down'].shape[0]
    main_w = 2 * kw + 2 * gw
    row = lambda t: t.reshape(1, -1)

    w_in_bf = w['w_in'].astype(BF16)
    w_u_bf = w['w_in'][:, main_w + rank:].astype(BF16)
    w_a_bf = w['w_in'][:, main_w:main_w + rank].astype(BF16)
    w_glu_bf = w['w_glu'].astype(BF16)
    w_out_bf = w['w_out'].astype(BF16)
    w_up_bf = w['w_up'].astype(BF16)
    w_down_bf = w['w_down'].astype(BF16)

    n_c = b + nb
    n_c_pad = -(-n_c // 8) * 8
    c_all = jnp.concatenate([cp, cs, jnp.zeros((n_c_pad - n_c, d), F32)], axis=0)
    mod = _ada(c_all, w['w_ada'], row(w['b_ada']))
    mod_p = mod[:b].reshape(b, 1, 6 * d)
    mod_s = mod[b:n_c]

    xp2 = xp.reshape(b * l, d)
    xs2 = jnp.swapaxes(xs, 0, 1).reshape(seq * nb, d)
    tm_p = min(TM_PROMPT, l)
    tps = l // tm_p
    seq_of = lambda i: i // tps

    proj = functools.partial(_inproj, norm1=row(w['norm1']), w_in_bf=w_in_bf, w_u_bf=w_u_bf, w_a_bf=w_a_bf,
                             w_a2=w['w_a2'], b_a2=row(w['b_a2']), main_w=main_w)
    qkvg_p, u_p, la_p = proj(xp2, mod_p, seq_of, tm=tm_p, n_slabs=1, main_dtype=BF16)
    qkvg_s, u_s, la_s = proj(xs2, mod_s, None, tm=seq * nb, n_slabs=seq, main_dtype=F32)

    gn = row(w['gla_norm'])
    o_p, gla_p = _gla_prompt(qkvg_p, la_p, gn, b=b, l=l, h=h, dk=dk, dv=dv)
    o_s, gla_s = _gla_sample(qkvg_s, la_s, gn, st_gla, nb=nb, seq=seq, h=h, dk=dk, dv=dv)

    s5w = (w['s5_lam_re'], w['s5_lam_im'], w['s5_log_step'], w['s5_b_re'], w['s5_b_im'],
           w['s5_c_re'], w['s5_c_im'], w['s5_d'])
    cp_ = math.gcd(l, S5_CHUNK)
    n_chunk = l // cp_
    m_p, wst_p, v_p, lc_p = _s5_prep(*s5w, c_mat=cp_, c_real=cp_)
    y5_p, re_p, im_p = _s5_prompt(u_p, m_p, wst_p, v_p, lc_p, n_seq=b, n_chunk=n_chunk, c=cp_, ch=ch)
    re_p, im_p = re_p.reshape(b, g, p), im_p.reshape(b, g, p)

    m_s, wst_s, v_s, lc_s = _s5_prep(*s5w, c_mat=S5_CHUNK_SAMPLE, c_real=seq)
    y5_s, re_s, im_s = _s5_sample(u_s, st_re.reshape(nb, g * p), st_im.reshape(nb, g * p), m_s, wst_s, v_s, lc_s,
                                  seq=seq, nb=nb, ch=ch)
    re_s, im_s = re_s.reshape(nb, g, p), im_s.reshape(nb, g, p)

    mixer = functools.partial(_mix, norm2=row(w['norm2']), w_glu_bf=w_glu_bf, b_glu=row(w['b_glu']),
                              w_out_bf=w_out_bf)
    tm_mix = min(TM_MIX, l)
    x1_p, h2_p = mixer(o_p, y5_p, xp2, mod_p, lambda i: i // (l // tm_mix), tm=tm_mix, n_slabs=1)
    x1_s, h2_s = mixer(o_s, y5_s, xs2, mod_s, None, tm=seq * nb, n_slabs=seq)

    fn = row(final_norm)
    conv_w, conv_b = w['conv_w'], row(w['conv_b'])
    yp, cs_p = _ffn_prompt(h2_p, x1_p, mod_p, fn, w_up_bf, conv_w, conv_b, w_down_bf, b=b, l=l, tm=tm_p)
    conv_p = cs_p[tps - 1::tps].reshape(b, 2, 2 * dff)
    st_slab = jnp.swapaxes(st_conv, 0, 1).reshape(2 * nb, 2 * dff)
    ys, csa, csg = _ffn_sample(h2_s, x1_s, mod_s, fn, st_slab, w_up_bf, conv_w, conv_b, w_down_bf,
                               nb=nb, seq=seq)
    conv_s = jnp.swapaxes(jnp.concatenate([csa, csg], axis=-1).reshape(2, nb, 2 * dff), 0, 1)

    yp = yp.reshape(b, l, d)
    ys = jnp.swapaxes(ys.reshape(seq, nb, d), 0, 1)
    return yp, ys, (gla_p, re_p, im_p, conv_p), (gla_s, re_s, im_s, conv_s)


def kernel(x_prompt, x_sample, c_prompt, c_sample, state_gla, state_s5_re, state_s5_im, state_conv, w_ada, b_ada, norm1, w_in, w_a2, b_a2, gla_norm, s5_lam_re, s5_lam_im, s5_log_step, s5_b_re, s5_b_im, s5_c_re, s5_c_im, s5_d, w_glu, b_glu, w_out, norm2, w_up, conv_w, conv_b, w_down, final_norm):
    depth = w_ada.shape[0]
    assert depth == 1, "the final norm is fused into the last layer's FFN; only depth 1 is wired up"
    w = dict(w_ada=w_ada[0], b_ada=b_ada[0], norm1=norm1[0], w_in=w_in[0], w_a2=w_a2[0], b_a2=b_a2[0],
             gla_norm=gla_norm[0], s5_lam_re=s5_lam_re[0], s5_lam_im=s5_lam_im[0], s5_log_step=s5_log_step[0],
             s5_b_re=s5_b_re[0], s5_b_im=s5_b_im[0], s5_c_re=s5_c_re[0], s5_c_im=s5_c_im[0], s5_d=s5_d[0],
             w_glu=w_glu[0], b_glu=b_glu[0], w_out=w_out[0], norm2=norm2[0], w_up=w_up[0], conv_w=conv_w[0],
             conv_b=conv_b[0], w_down=w_down[0])
    yp, ys, sp, ss = _layer(x_prompt, x_sample, c_prompt, c_sample, state_gla[0], state_s5_re[0],
                            state_s5_im[0], state_conv[0], w, final_norm)
    stack = lambda t: t[None]
    return (yp, ys, stack(sp[0]), stack(sp[1]), stack(sp[2]), stack(sp[3]),
            stack(ss[0]), stack(ss[1]), stack(ss[2]), stack(ss[3]))
```

```python
import functools
import math

import jax
import jax.numpy as jnp
from jax import lax
from jax.experimental import pallas as pl
from jax.experimental.pallas import tpu as pltpu

F32 = jnp.float32
BF16 = jnp.bfloat16

NORM_EPS = 1e-6
GLA_TAU = 16.0
GLA_CHUNK = 64
GLA_SUB = 8
LOG2_E = 1.4426950408889634
GLA_SEQ_BLOCK = 512
LANES = 128
GLA_PAD = 128
S5_CHUNK = 16
S5_CHUNK_SAMPLE = 8
VMEM_LIMIT_BYTES = 56 * 1024 * 1024
TM_PROMPT = 512
TM_MIX = 512
FFN_TF = 512
FFN_TN = 512


def _cparams(*sem):
    return pltpu.CompilerParams(dimension_semantics=sem, vmem_limit_bytes=VMEM_LIMIT_BYTES)


def _dot(a, b):
    return jnp.dot(a.astype(BF16), b.astype(BF16), preferred_element_type=F32)


def _dot_nt(a, b):
    return lax.dot_general(a.astype(BF16), b.astype(BF16), (((1,), (1,)), ((), ())),
                           preferred_element_type=F32)


def _dot_f32(a, b):
    return jnp.dot(a, b, preferred_element_type=F32, precision=lax.Precision.HIGHEST)


def _rms(x):
    return x * lax.rsqrt(jnp.mean(x * x, axis=-1, keepdims=True) + NORM_EPS)


def _gelu(x):
    return 0.5 * x * (1.0 + jnp.tanh(math.sqrt(2.0 / math.pi) * (x + 0.044715 * (x * x * x))))


def _sigmoid(x):
    return 1.0 / (1.0 + jnp.exp(-x))


def _ada_kernel(c_ref, w_ref, b_ref, o_ref):
    c = c_ref[...]
    o_ref[...] = _dot(c * _sigmoid(c), w_ref[...]) + b_ref[...]


def _ada(c_all, w_ada, b_ada, tn=1024):
    m, d = c_all.shape
    n = w_ada.shape[1]
    return pl.pallas_call(
        _ada_kernel,
        grid=(n // tn,),
        in_specs=[pl.BlockSpec((m, d), lambda j: (0, 0)),
                  pl.BlockSpec((d, tn), lambda j: (0, j)),
                  pl.BlockSpec((1, tn), lambda j: (0, j))],
        out_specs=pl.BlockSpec((m, tn), lambda j: (0, j)),
        out_shape=jax.ShapeDtypeStruct((m, n), F32),
        compiler_params=_cparams("arbitrary"),
        name="ada",
    )(c_all, w_ada, b_ada)


def _resident(shape, index_map):
    return pl.BlockSpec(shape, index_map, pipeline_mode=pl.Buffered(1))


def _inproj_kernel(x_ref, sh_ref, sc_ref, n1_ref, wm_ref, wu_ref, wa_ref, wa2_ref, ba2_ref,
                   qkvg_ref, u_ref, la_ref, h_scr, *, n_slabs, tn):
    rows = x_ref.shape[0] // n_slabs
    for l in range(n_slabs):
        sl = slice(l * rows, (l + 1) * rows)
        h = _rms(x_ref[sl, :]) * n1_ref[...] * (1.0 + sc_ref[...]) + sh_ref[...]
        h_scr[sl, :] = h.astype(BF16)
    a_lr = jnp.dot(h_scr[...], wa_ref[...], preferred_element_type=F32)
    z = _dot(a_lr, wa2_ref[...]) + ba2_ref[...]
    la_ref[...] = (jnp.minimum(z, 0.0) - jnp.log(1.0 + jnp.exp(-jnp.abs(z)))) / GLA_TAU
    for out_ref, w_ref in ((qkvg_ref, wm_ref), (u_ref, wu_ref)):
        for c0 in range(0, out_ref.shape[1], tn):
            part = jnp.dot(h_scr[...], w_ref[:, c0:c0 + tn], preferred_element_type=F32)
            out_ref[:, c0:c0 + tn] = part.astype(out_ref.dtype)


def _inproj(x, mod, mod_row, norm1, w_in_bf, w_u_bf, w_a_bf, w_a2, b_a2, *, tm, n_slabs, main_w, main_dtype,
            tn=512):
    r, d = x.shape
    sw = w_u_bf.shape[1]
    kw = w_a2.shape[1]
    rank = w_a_bf.shape[1]
    if mod.ndim == 3:
        mspec = lambda k: pl.BlockSpec((None, 1, d), lambda i: (mod_row(i), 0, k))
    else:
        mspec = lambda k: pl.BlockSpec((mod.shape[0], d), lambda i: (0, k))
    kern = functools.partial(_inproj_kernel, n_slabs=n_slabs, tn=tn)
    return pl.pallas_call(
        kern,
        grid=(r // tm,),
        in_specs=[pl.BlockSpec((tm, d), lambda i: (i, 0)),
                  mspec(0), mspec(1),
                  pl.BlockSpec((1, d), lambda i: (0, 0)),
                  _resident((d, main_w), lambda i: (0, 0)),
                  _resident((d, sw), lambda i: (0, 0)),
                  _resident((d, rank), lambda i: (0, 0)),
                  pl.BlockSpec((rank, kw), lambda i: (0, 0)),
                  pl.BlockSpec((1, kw), lambda i: (0, 0))],
        out_specs=[pl.BlockSpec((tm, main_w), lambda i: (i, 0)),
                   pl.BlockSpec((tm, sw), lambda i: (i, 0)),
                   pl.BlockSpec((tm, kw), lambda i: (i, 0))],
        out_shape=[jax.ShapeDtypeStruct((r, main_w), main_dtype),
                   jax.ShapeDtypeStruct((r, sw), F32),
                   jax.ShapeDtypeStruct((r, kw), F32)],
        scratch_shapes=[pltpu.VMEM((tm, d), BF16)],
        compiler_params=_cparams("arbitrary"),
        name="inproj",
    )(x, mod, mod, norm1, w_in_bf, w_u_bf, w_a_bf, w_a2, b_a2)


def _gla_chunk(q, k, v, ga, states, *, h, sub):
    c = q.shape[0]
    dk, dv = q.shape[1] // h, v.shape[1] // h
    rows = lax.broadcasted_iota(jnp.int32, (c, c), 0)
    cols = lax.broadcasted_iota(jnp.int32, (c, c), 1)
    sums, widths = [rows >= cols], []
    w = sub
    while w < c:
        sums.append(cols <= (rows // (2 * w)) * (2 * w) + (w - 1))
        widths.append(w)
        w *= 2
    sums.append(cols >= 0)
    pmat = jnp.concatenate(sums, axis=0).astype(F32).astype(BF16)
    ga_hi = ga.astype(BF16)
    ga_lo = (ga - ga_hi.astype(F32)).astype(BF16)
    tot = (jnp.dot(pmat, ga_hi, preferred_element_type=F32)
           + jnp.dot(pmat, ga_lo, preferred_element_type=F32))
    tot = tot * LOG2_E
    cum = tot[:c]
    last = tot[(len(sums) - 1) * c:]

    row_w = lax.broadcasted_iota(jnp.int32, q.shape, 0)
    lane_o = lax.broadcasted_iota(jnp.int32, (c, LANES), 1)
    by_off = [jnp.zeros((c, LANES), F32) for _ in range(h)]
    for off in range(sub):
        k_sh = k if off == 0 else pltpu.roll(k, off, axis=0)
        c_sh = cum if off == 0 else pltpu.roll(cum, off, axis=0)
        prod = q * k_sh * jnp.exp2(jnp.minimum(cum - c_sh, 0.0))
        for hh in range(h):
            col = jnp.sum(prod[:, hh * dk:(hh + 1) * dk], axis=-1, keepdims=True)
            by_off[hh] = jnp.where(lane_o == c - 1 - off, col, by_off[hh])
    same_sub = rows // sub == cols // sub
    att = []
    for hh in range(h):
        moved = pltpu.roll(by_off[hh], LANES - (c - 1), axis=1, stride=1, stride_axis=0)
        att.append(jnp.where(same_sub, moved[:, :c], 0.0))
    for lvl, w in enumerate(widths):
        ref = tot[(lvl + 1) * c:(lvl + 2) * c]
        odd = (row_w // w) % 2 == 1
        x = jnp.where(odd, q, k) * jnp.exp2(jnp.where(odd, cum - ref, ref - cum))
        pair = (rows // (2 * w) == cols // (2 * w)) & ((rows // w) % 2 == 1) & ((cols // w) % 2 == 0)
        for hh in range(h):
            xh = x[:, hh * dk:(hh + 1) * dk]
            att[hh] = att[hh] + jnp.where(pair, _dot_nt(xh, xh), 0.0)

    q_in = q * jnp.exp2(cum)
    k_out = k * jnp.exp2(last - cum)
    e_last = jnp.exp2(last[0:1])
    tail_row = lax.broadcasted_iota(jnp.int32, (GLA_PAD - c, dk), 0)
    outs, new_states = [], []
    for hh in range(h):
        ck, cv = slice(hh * dk, (hh + 1) * dk), slice(hh * dv, (hh + 1) * dv)
        outs.append(_dot(att[hh], v[:, cv]) + _dot(q_in[:, ck], states[hh]))
        tail = jnp.where(tail_row == 0, e_last[:, ck], 0.0)
        kt = jnp.concatenate([k_out[:, ck], tail], axis=0).T
        new_states.append(states[hh] * kt[:, c:c + 1] + _dot(kt[:, :c], v[:, cv]))
    return outs, new_states


def _gla_finish(o, g, gn):
    return _rms(o) * gn * (g * _sigmoid(g))


def _gla_prompt_kernel(q_ref, k_ref, v_ref, g_ref, la_ref, gn_ref, o_ref, s_ref, s_scr, *, h, dk, dv, chunk, sub,
                       scale):
    blk = pl.program_id(1)

    @pl.when(blk == 0)
    def _():
        s_scr[...] = jnp.zeros_like(s_scr)

    def body(n, carry):
        r = pl.ds(pl.multiple_of(n * chunk, chunk), chunk)
        outs, new_states = _gla_chunk(q_ref[r, :].astype(F32) * scale, k_ref[r, :].astype(F32),
                                      v_ref[r, :].astype(F32), la_ref[r, :],
                                      [s_scr[hh] for hh in range(h)], h=h, sub=sub)
        for hh in range(h):
            cv = slice(hh * dv, (hh + 1) * dv)
            s_scr[hh] = new_states[hh]
            o_ref[r, cv] = _gla_finish(outs[hh], g_ref[r, cv].astype(F32), gn_ref[...]).astype(o_ref.dtype)
        return carry

    lax.fori_loop(0, q_ref.shape[0] // chunk, body, 0)

    @pl.when(blk == pl.num_programs(1) - 1)
    def _():
        s_ref[...] = s_scr[...]


def _gla_prompt(qkvg, la, gla_norm, *, b, l, h, dk, dv):
    kw, gw = h * dk, h * dv
    chunk = math.gcd(l, GLA_CHUNK)
    sub = math.gcd(chunk, GLA_SUB)
    lb = math.gcd(l, GLA_SEQ_BLOCK)
    nlb = l // lb
    kern = functools.partial(_gla_prompt_kernel, h=h, dk=dk, dv=dv, chunk=chunk, sub=sub, scale=dk ** -0.5)
    return pl.pallas_call(
        kern,
        grid=(b, nlb),
        in_specs=[pl.BlockSpec((lb, kw), lambda i, j: (i * nlb + j, 0)),
                  pl.BlockSpec((lb, kw), lambda i, j: (i * nlb + j, 1)),
                  pl.BlockSpec((lb, gw), lambda i, j: (i * nlb + j, 2 * kw // gw)),
                  pl.BlockSpec((lb, gw), lambda i, j: (i * nlb + j, 2 * kw // gw + 1)),
                  pl.BlockSpec((lb, kw), lambda i, j: (i * nlb + j, 0)),
                  pl.BlockSpec((1, dv), lambda i, j: (0, 0))],
        out_specs=[pl.BlockSpec((lb, gw), lambda i, j: (i * nlb + j, 0)),
                   pl.BlockSpec((None, h, dk, dv), lambda i, j: (i, 0, 0, 0))],
        out_shape=[jax.ShapeDtypeStruct((b * l, gw), BF16),
                   jax.ShapeDtypeStruct((b, h, dk, dv), F32)],
        scratch_shapes=[pltpu.VMEM((h, dk, dv), F32)],
        compiler_params=_cparams("arbitrary", "arbitrary"),
        name="gla_prompt",
    )(qkvg, qkvg, qkvg, qkvg, la, gla_norm)


def _gla_sample_kernel(qkvg_ref, la_ref, gn_ref, s0_ref, o_ref, s_ref, *, nb, seq, h, dk, dv, pad, scale):
    bt = s0_ref.shape[0]
    kw, gw = h * dk, h * dv
    in_w = 2 * kw + 2 * gw
    i = pl.program_id(0)

    def rows_of(b, width, col):
        per_row = width // LANES
        return pl.ds(b * per_row + col // LANES, seq, stride=nb * per_row)

    def gather(ref, b, width, col, n_col):
        parts = [ref[rows_of(b, width, col + c), :] for c in range(0, n_col, LANES)]
        x = parts[0] if len(parts) == 1 else jnp.concatenate(parts, axis=1)
        return jnp.concatenate([x, jnp.zeros((pad - seq, n_col), F32)], axis=0)

    def body(bb, carry):
        b = i * bt + bb
        q = gather(qkvg_ref, b, in_w, 0, kw) * scale
        k = gather(qkvg_ref, b, in_w, kw, kw)
        v = gather(qkvg_ref, b, in_w, 2 * kw, gw)
        g = gather(qkvg_ref, b, in_w, 2 * kw + gw, gw)
        ga = gather(la_ref, b, kw, 0, kw)
        outs, new_states = _gla_chunk(q, k, v, ga, [s0_ref[bb, hh] for hh in range(h)], h=h, sub=pad)
        for hh in range(h):
            s_ref[bb, hh] = new_states[hh]
            res = _gla_finish(outs[hh], g[:, hh * dv:(hh + 1) * dv], gn_ref[...])
            for c in range(0, dv, LANES):
                o_ref[rows_of(b, gw, hh * dv + c), :] = res[:seq, c:c + LANES]
        return carry

    lax.fori_loop(0, bt, body, 0, unroll=4)


def _gla_sample(qkvg, la, gla_norm, s0, *, nb, seq, h, dk, dv, bt=8):
    gw = h * dv
    pad = 8
    assert dk % LANES == 0 and dv % LANES == 0
    qkvg, la = qkvg.reshape(-1, LANES), la.reshape(-1, LANES)
    o_rows = nb * seq * gw // LANES
    kern = functools.partial(_gla_sample_kernel, nb=nb, seq=seq, h=h, dk=dk, dv=dv, pad=pad, scale=dk ** -0.5)
    o, s_new = pl.pallas_call(
        kern,
        grid=(nb // bt,),
        in_specs=[pl.BlockSpec(qkvg.shape, lambda i: (0, 0)),
                  pl.BlockSpec(la.shape, lambda i: (0, 0)),
                  pl.BlockSpec((1, dv), lambda i: (0, 0)),
                  pl.BlockSpec((bt, h, dk, dv), lambda i: (i, 0, 0, 0))],
        out_specs=[pl.BlockSpec((o_rows, LANES), lambda i: (0, 0)),
                   pl.BlockSpec((bt, h, dk, dv), lambda i: (i, 0, 0, 0))],
        out_shape=[jax.ShapeDtypeStruct((o_rows, LANES), F32),
                   jax.ShapeDtypeStruct((nb, h, dk, dv), F32)],
        compiler_params=_cparams("arbitrary"),
        name="gla_sample",
    )(qkvg, la, gla_norm, s0)
    return o.reshape(nb * seq, gw), s_new


def _s5_prep_kernel(lamc_ref, lamr_ref, ls_ref, bt_re_ref, bt_im_ref, btile_re_ref, btile_im_ref,
                    ctile_re_ref, ctile_im_ref, d_ref, m_ref, wst_ref, v_ref, lc_ref, *, ch, c_mat, c_real):
    w = ch * c_mat
    p = lamc_ref.shape[1]
    lane_m = lax.broadcasted_iota(jnp.int32, (p, LANES), 1)
    m_f = jnp.where(lane_m <= c_mat, lane_m, 0).astype(F32)
    sel = lax.broadcasted_iota(jnp.int32, (LANES, w), 0)
    tau = lax.broadcasted_iota(jnp.int32, (LANES, w), 1) // ch
    spread_pw = (sel == tau).astype(F32)
    spread_pv = (sel == tau + 1).astype(F32)
    spread_ps = (sel == jnp.maximum(c_real - 1 - tau, 0)).astype(F32)
    rr = lax.broadcasted_iota(jnp.int32, (ch, w), 0)
    cc = lax.broadcasted_iota(jnp.int32, (ch, w), 1)
    spread_ch = (cc % ch == rr).astype(F32)

    def bbar_coef(lam_re, lam_im, lb_re, lb_im):
        den = lam_re * lam_re + lam_im * lam_im
        x, y = lb_re - 1.0, lb_im
        return (x * lam_re + y * lam_im) / den, (y * lam_re - x * lam_im) / den

    for gi in range(m_ref.shape[0]):
        dt = jnp.exp(ls_ref[gi])
        lam_re, lam_im = lamc_ref[gi, :, 0:1], lamc_ref[gi, :, 1:2]
        mag = jnp.exp(lam_re * dt * m_f)
        t_re, t_im = mag * jnp.cos(lam_im * dt * m_f), mag * jnp.sin(lam_im * dt * m_f)
        cf_re, cf_im = bbar_coef(lam_re, lam_im, t_re[:, 1:2], t_im[:, 1:2])

        ct_re, ct_im = _dot_f32(ctile_re_ref[gi], spread_ch), _dot_f32(ctile_im_ref[gi], spread_ch)
        bt_re, bt_im = _dot_f32(btile_re_ref[gi], spread_ch), _dot_f32(btile_im_ref[gi], spread_ch)
        pw_re, pw_im = _dot_f32(t_re, spread_pw), _dot_f32(t_im, spread_pw)
        cl_re = ct_re * pw_re - ct_im * pw_im
        cl_im = ct_re * pw_im + ct_im * pw_re
        pv_re, pv_im = _dot_f32(t_re, spread_pv), _dot_f32(t_im, spread_pv)
        v_ref[gi, 0:p, :] = (ct_re * pv_re - ct_im * pv_im).astype(v_ref.dtype)
        v_ref[gi, p:2 * p, :] = (-(ct_re * pv_im + ct_im * pv_re)).astype(v_ref.dtype)

        bb_re = cf_re * bt_re - cf_im * bt_im
        bb_im = cf_re * bt_im + cf_im * bt_re
        ps_re, ps_im = _dot_f32(t_re, spread_ps), _dot_f32(t_im, spread_ps)
        wst_ref[gi, 0:p, :] = (bb_re * ps_re - bb_im * ps_im).astype(wst_ref.dtype)
        wst_ref[gi, p:2 * p, :] = (bb_re * ps_im + bb_im * ps_re).astype(wst_ref.dtype)

        lr_re, lr_im = lamr_ref[gi, 0:1, :], lamr_ref[gi, 1:2, :]
        ar, tr = lr_re * dt, lr_im * dt
        ea = jnp.exp(ar)
        rf_re, rf_im = bbar_coef(lr_re, lr_im, ea * jnp.cos(tr), ea * jnp.sin(tr))
        bbt_re = rf_re * bt_re_ref[gi] - rf_im * bt_im_ref[gi]
        bbt_im = rf_re * bt_im_ref[gi] + rf_im * bt_re_ref[gi]
        kcat = _dot_f32(bbt_re, cl_re) - _dot_f32(bbt_im, cl_im)
        kcat = kcat + jnp.where(rr == cc, d_ref[gi], 0.0)
        for i in range(c_mat):
            shifted = kcat if i == 0 else pltpu.roll(kcat, ch * i, axis=1)
            m_ref[gi, i * ch:(i + 1) * ch, :] = jnp.where(cc >= ch * i, shifted, 0.0).astype(m_ref.dtype)

        ec = jnp.exp(ar * float(c_real))
        lc_ref[gi, 0:1, :] = ec * jnp.cos(tr * float(c_real))
        lc_ref[gi, 1:2, :] = ec * jnp.sin(tr * float(c_real))


def _s5_prep(lam_re, lam_im, log_step, b_re, b_im, c_re, c_im, d_skip, *, c_mat, c_real):
    g, p, ch = b_re.shape
    w = ch * c_mat
    gb = LANES // ch
    lam_col = jnp.stack([lam_re, lam_im], axis=-1)
    lam_row = jnp.stack([lam_re, lam_im], axis=1)
    ls = log_step.reshape(g, 1, 1)
    bt = lambda t: jnp.swapaxes(t, 1, 2)
    d_row = jnp.pad(d_skip.reshape(g, 1, ch), ((0, 0), (0, 0), (0, w - ch)))
    spec3 = lambda s: pl.BlockSpec((gb,) + s, lambda i: (i, 0, 0))
    kern = functools.partial(_s5_prep_kernel, ch=ch, c_mat=c_mat, c_real=c_real)
    return pl.pallas_call(
        kern,
        grid=(g // gb,),
        in_specs=[spec3((p, 2)), spec3((2, p)), spec3((1, 1)), spec3((ch, p)), spec3((ch, p)),
                  spec3((p, ch)), spec3((p, ch)), spec3((p, ch)), spec3((p, ch)), spec3((1, w))],
        out_specs=[spec3((w, w)), spec3((2 * p, w)), spec3((2 * p, w)), spec3((2, p))],
        out_shape=[jax.ShapeDtypeStruct((g, w, w), BF16), jax.ShapeDtypeStruct((g, 2 * p, w), BF16),
                   jax.ShapeDtypeStruct((g, 2 * p, w), BF16), jax.ShapeDtypeStruct((g, 2, p), F32)],
        compiler_params=_cparams("arbitrary"),
        name="s5_prep",
    )(lam_col, lam_row, ls, bt(b_re), bt(b_im), b_re, b_im, bt(c_re), bt(c_im), d_row)


def _cmul_rows(z, lr, li):
    p = lr.shape[1]
    coef_a = jnp.concatenate([lr, lr], axis=1)
    coef_b = jnp.concatenate([-li, li], axis=1)
    return z * coef_a + pltpu.roll(z, p, axis=1) * coef_b


def _regroup(parts, gi, ch):
    return jnp.concatenate([t[:, gi * ch:(gi + 1) * ch] for t in parts], axis=1)


def _s5_prompt_kernel(u_ref, m_ref, wst_ref, v_ref, lc_ref, y_ref, hre_ref, him_ref, *, ch, c, n_seq, n_chunk):
    gb = m_ref.shape[0]
    r = n_seq * n_chunk
    p = lc_ref.shape[2]
    toks = [pltpu.bitcast(u_ref[pl.ds(i, r, stride=c), :].astype(BF16), jnp.uint32) for i in range(c)]
    pos = lax.broadcasted_iota(jnp.int32, (r, 2 * p), 0) % n_chunk
    ys, h_re, h_im = [], [], []
    for gi in range(gb):
        x = pltpu.bitcast(_regroup(toks, gi, ch), BF16)
        z = lax.dot_general(x, wst_ref[gi], (((1,), (1,)), ((), ())), preferred_element_type=F32)
        lr, li = lc_ref[gi, 0:1, :], lc_ref[gi, 1:2, :]
        d = 1
        while d < n_chunk:
            zs = jnp.where(pos >= d, pltpu.roll(z, d, axis=0), 0.0)
            z = z + _cmul_rows(zs, lr, li)
            lr, li = lr * lr - li * li, 2.0 * lr * li
            d *= 2
        h_in = jnp.where(pos >= 1, pltpu.roll(z, 1, axis=0), 0.0)
        y = (jnp.dot(x, m_ref[gi], preferred_element_type=F32)
             + jnp.dot(h_in.astype(BF16), v_ref[gi], preferred_element_type=F32))
        ys.append(pltpu.bitcast(y.astype(BF16), jnp.uint32))
        hf = jnp.concatenate([z[(s + 1) * n_chunk - 1:(s + 1) * n_chunk] for s in range(n_seq)], axis=0)
        h_re.append(hf[:, :p])
        h_im.append(hf[:, p:])
    for j in range(c):
        y_ref[pl.ds(j, r, stride=c), :] = pltpu.bitcast(_regroup(ys, j, ch), BF16).astype(F32)
    hre_ref[...] = jnp.concatenate(h_re, axis=1)
    him_ref[...] = jnp.concatenate(h_im, axis=1)


def _s5_prompt(u, m, wst, v, lc, *, n_seq, n_chunk, c, ch):
    t, sw = u.shape
    g, w, _ = m.shape
    p2 = wst.shape[1]
    p = p2 // 2
    gb = LANES // ch
    spec3 = lambda s: pl.BlockSpec((gb,) + s, lambda i: (i, 0, 0))
    kern = functools.partial(_s5_prompt_kernel, ch=ch, c=c, n_seq=n_seq, n_chunk=n_chunk)
    return pl.pallas_call(
        kern,
        grid=(g // gb,),
        in_specs=[pl.BlockSpec((t, LANES), lambda i: (0, i)),
                  spec3((w, w)), spec3((p2, w)), spec3((p2, w)), spec3((2, p))],
        out_specs=[pl.BlockSpec((t, LANES), lambda i: (0, i)),
                   pl.BlockSpec((n_seq, gb * p), lambda i: (0, i)),
                   pl.BlockSpec((n_seq, gb * p), lambda i: (0, i))],
        out_shape=[jax.ShapeDtypeStruct((t, sw), F32), jax.ShapeDtypeStruct((n_seq, g * p), F32),
                   jax.ShapeDtypeStruct((n_seq, g * p), F32)],
        compiler_params=_cparams("arbitrary"),
        name="s5_prompt",
    )(u, m, wst, v, lc)


def _s5_sample_kernel(u_ref, sre_ref, sim_ref, m_ref, wst_ref, v_ref, lc_ref, y_ref, hre_ref, him_ref, *,
                      ch, seq, nb):
    gb = m_ref.shape[0]
    p = lc_ref.shape[2]
    w = m_ref.shape[1]
    toks = [u_ref[l * nb:(l + 1) * nb, :] for l in range(seq)]
    ys, h_re, h_im = [], [], []
    for gi in range(gb):
        x = jnp.concatenate([_regroup(toks, gi, ch), jnp.zeros((nb, w - seq * ch), F32)], axis=1).astype(BF16)
        h0 = jnp.concatenate([sre_ref[:, gi * p:(gi + 1) * p], sim_ref[:, gi * p:(gi + 1) * p]], axis=1)
        ys.append(jnp.dot(x, m_ref[gi], preferred_element_type=F32)
                  + jnp.dot(h0.astype(BF16), v_ref[gi], preferred_element_type=F32))
        hf = (_cmul_rows(h0, lc_ref[gi, 0:1, :], lc_ref[gi, 1:2, :])
              + lax.dot_general(x, wst_ref[gi], (((1,), (1,)), ((), ())), preferred_element_type=F32))
        h_re.append(hf[:, :p])
        h_im.append(hf[:, p:])
    for l in range(seq):
        y_ref[l * nb:(l + 1) * nb, :] = _regroup(ys, l, ch)
    hre_ref[...] = jnp.concatenate(h_re, axis=1)
    him_ref[...] = jnp.concatenate(h_im, axis=1)


def _s5_sample(u, st_re, st_im, m, wst, v, lc, *, seq, nb, ch):
    t, sw = u.shape
    g, w, _ = m.shape
    p2 = wst.shape[1]
    p = p2 // 2
    gb = LANES // ch
    spec3 = lambda s: pl.BlockSpec((gb,) + s, lambda i: (i, 0, 0))
    sspec = pl.BlockSpec((nb, gb * p), lambda i: (0, i))
    kern = functools.partial(_s5_sample_kernel, ch=ch, seq=seq, nb=nb)
    return pl.pallas_call(
        kern,
        grid=(g // gb,),
        in_specs=[pl.BlockSpec((t, LANES), lambda i: (0, i)), sspec, sspec,
                  spec3((w, w)), spec3((p2, w)), spec3((p2, w)), spec3((2, p))],
        out_specs=[pl.BlockSpec((t, LANES), lambda i: (0, i)), sspec, sspec],
        out_shape=[jax.ShapeDtypeStruct((t, sw), F32), jax.ShapeDtypeStruct((nb, g * p), F32),
                   jax.ShapeDtypeStruct((nb, g * p), F32)],
        compiler_params=_cparams("arbitrary"),
        name="s5_sample",
    )(u, st_re, st_im, m, wst, v, lc)


def _mix_kernel(o_ref, y_ref, x_ref, g1_ref, sh2_ref, sc2_ref, n2_ref, wglu_ref, bglu_ref, wo_ref,
                x1_ref, h2_ref, *, n_slabs, tn):
    rows = x_ref.shape[0] // n_slabs
    gw = o_ref.shape[1]
    yg = _gelu(y_ref[...])
    z = (yg * _sigmoid(_dot(yg, wglu_ref[...]) + bglu_ref[...])).astype(BF16)
    ob = o_ref[...].astype(BF16)
    for c0 in range(0, x_ref.shape[1], tn):
        cols = slice(c0, c0 + tn)
        mix = (jnp.dot(ob, wo_ref[0:gw, cols], preferred_element_type=F32)
               + jnp.dot(z, wo_ref[gw:, cols], preferred_element_type=F32))
        for l in range(n_slabs):
            sl = slice(l * rows, (l + 1) * rows)
            x1_ref[sl, cols] = x_ref[sl, cols] + g1_ref[:, cols] * mix[sl, :]
    for l in range(n_slabs):
        sl = slice(l * rows, (l + 1) * rows)
        h2 = _rms(x1_ref[sl, :]) * n2_ref[...] * (1.0 + sc2_ref[...]) + sh2_ref[...]
        h2_ref[sl, :] = h2.astype(BF16)


def _mix(o, y, x, mod, mod_row, norm2, w_glu_bf, b_glu, w_out_bf, *, tm, n_slabs, tn=512):
    r, d = x.shape
    gw = o.shape[1]
    sw = y.shape[1]
    if mod.ndim == 3:
        mspec = lambda k: pl.BlockSpec((None, 1, d), lambda i: (mod_row(i), 0, k))
    else:
        mspec = lambda k: pl.BlockSpec((mod.shape[0], d), lambda i: (0, k))
    kern = functools.partial(_mix_kernel, n_slabs=n_slabs, tn=tn)
    return pl.pallas_call(
        kern,
        grid=(r // tm,),
        in_specs=[pl.BlockSpec((tm, gw), lambda i: (i, 0)),
                  pl.BlockSpec((tm, sw), lambda i: (i, 0)),
                  pl.BlockSpec((tm, d), lambda i: (i, 0)),
                  mspec(2), mspec(3), mspec(4),
                  pl.BlockSpec((1, d), lambda i: (0, 0)),
                  _resident((sw, sw), lambda i: (0, 0)),
                  pl.BlockSpec((1, sw), lambda i: (0, 0)),
                  _resident((gw + sw, d), lambda i: (0, 0))],
        out_specs=[pl.BlockSpec((tm, d), lambda i: (i, 0)),
                   pl.BlockSpec((tm, d), lambda i: (i, 0))],
        out_shape=[jax.ShapeDtypeStruct((r, d), F32), jax.ShapeDtypeStruct((r, d), BF16)],
        compiler_params=_cparams("arbitrary"),
        name="mix",
    )(o, y, x, mod, mod, mod, norm2, w_glu_bf, b_glu, w_out_bf)


def _conv3(up, ext, cw_ref, cb_ref, off1, off2):
    n = up.shape[0]
    return (cw_ref[2:3, :] * up + cw_ref[1:2, :] * ext[off1:off1 + n] + cw_ref[0:1, :] * ext[off2:off2 + n]
            + cb_ref[...])


def _ffn_prompt_kernel(h_ref, halo_ref, x1_ref, g2_ref, fn_ref, wua_ref, wug_ref, cwa_ref, cwg_ref,
                       cba_ref, cbg_ref, wd_ref, y_ref, cs_ref, act_scr, x2_scr, *,
                       tiles_per_seq, halo, nf, tf, tn, sub_tf):
    i, j = pl.program_id(0), pl.program_id(1)

    @pl.when(j < nf)
    def _():
        first = (i % tiles_per_seq) == 0
        h = h_ref[...]
        hh = halo_ref[...]
        tm = h.shape[0]
        for c0 in range(0, tf, sub_tf):
            cols = slice(c0, c0 + sub_tf)

            def half(w_ref, cw_ref, cb_ref):
                up = jnp.dot(h, w_ref[:, cols], preferred_element_type=F32)
                up_halo = jnp.where(first, 0.0, jnp.dot(hh, w_ref[:, cols], preferred_element_type=F32))
                ext = jnp.concatenate([up_halo, up], axis=0)
                conv = (cw_ref[2:3, cols] * up + cw_ref[1:2, cols] * ext[halo - 1:halo - 1 + tm]
                        + cw_ref[0:1, cols] * ext[halo - 2:halo - 2 + tm] + cb_ref[:, cols])
                return conv, up

            a, up_a = half(wua_ref, cwa_ref, cba_ref)
            g, up_g = half(wug_ref, cwg_ref, cbg_ref)
            for rr in range(2):
                row = tm - 2 + rr
                cs_ref[rr, :, cols] = jnp.concatenate([up_a[row:row + 1], up_g[row:row + 1]], axis=0)
            act_scr[:, pl.ds(pl.multiple_of(j * tf + c0, sub_tf), sub_tf)] = (_gelu(a) * g).astype(BF16)

    @pl.when(j >= nf)
    def _():
        cols = pl.ds(pl.multiple_of((j - nf) * tn, tn), tn)
        ff = jnp.dot(act_scr[...], wd_ref[...], preferred_element_type=F32)
        x2_scr[:, cols] = x1_ref[...] + g2_ref[:, cols] * ff

    @pl.when(j == pl.num_programs(1) - 1)
    def _():
        y_ref[...] = _rms(x2_scr[...]) * fn_ref[...]


def _ffn_prompt(h2, x1, mod, final_norm, w_up_t, conv_w, conv_b, w_down_t, *, b, l, tm, halo=16):
    r, d = x1.shape
    nf, tf = w_up_t.shape[0] // 2, w_up_t.shape[2]
    nd, dff, tn = w_down_t.shape
    tps = l // tm
    up_j = lambda j: jnp.minimum(j, nf - 1)
    down_j = lambda j: jnp.maximum(j - nf, 0)
    kern = functools.partial(_ffn_prompt_kernel, tiles_per_seq=tps, halo=halo, nf=nf, tf=tf, tn=tn,
                             sub_tf=math.gcd(tf, 2 * LANES))
    return pl.pallas_call(
        kern,
        grid=(r // tm, nf + nd),
        in_specs=[pl.BlockSpec((tm, d), lambda i, j: (i, 0)),
                  pl.BlockSpec((halo, d), lambda i, j: (jnp.maximum(i * (tm // halo) - 1, 0), 0)),
                  pl.BlockSpec((tm, tn), lambda i, j: (i, down_j(j))),
                  pl.BlockSpec((None, 1, d), lambda i, j: (i // tps, 0, 5)),
                  pl.BlockSpec((1, d), lambda i, j: (0, 0)),
                  pl.BlockSpec((None, d, tf), lambda i, j: (up_j(j), 0, 0)),
                  pl.BlockSpec((None, d, tf), lambda i, j: (nf + up_j(j), 0, 0)),
                  pl.BlockSpec((3, tf), lambda i, j: (0, up_j(j))),
                  pl.BlockSpec((3, tf), lambda i, j: (0, nf + up_j(j))),
                  pl.BlockSpec((1, tf), lambda i, j: (0, up_j(j))),
                  pl.BlockSpec((1, tf), lambda i, j: (0, nf + up_j(j))),
                  pl.BlockSpec((None, dff, tn), lambda i, j: (down_j(j), 0, 0))],
        out_specs=[pl.BlockSpec((tm, d), lambda i, j: (i, 0)),
                   pl.BlockSpec((None, 2, 2, tf), lambda i, j: (i, 0, 0, up_j(j)))],
        out_shape=[jax.ShapeDtypeStruct((r, d), F32), jax.ShapeDtypeStruct((r // tm, 2, 2, dff), F32)],
        scratch_shapes=[pltpu.VMEM((tm, dff), BF16), pltpu.VMEM((tm, d), F32)],
        compiler_params=_cparams("arbitrary", "arbitrary"),
        name="ffn_prompt",
    )(h2, h2, x1, mod, final_norm, w_up_t, w_up_t, conv_w, conv_w, conv_b, conv_b, w_down_t)


def _ffn_sample_kernel(h_ref, x1_ref, g2_ref, fn_ref, st0a_ref, st1a_ref, st0g_ref, st1g_ref, wua_ref, wug_ref,
                       cwa_ref, cwg_ref, cba_ref, cbg_ref, wd_ref, y_ref, cs0a_ref, cs1a_ref, cs0g_ref, cs1g_ref,
                       act_scr, x2_scr, *, nb, seq, nf, tf, tn):
    j = pl.program_id(0)

    @pl.when(j < nf)
    def _():
        h = h_ref[...]

        def half(w_ref, st0_ref, st1_ref, cw_ref, cb_ref, cs0_ref, cs1_ref):
            up = jnp.dot(h, w_ref[...], preferred_element_type=F32)
            ext = jnp.concatenate([st0_ref[...], st1_ref[...], up], axis=0)
            cs0_ref[...] = up[(seq - 2) * nb:(seq - 1) * nb]
            cs1_ref[...] = up[(seq - 1) * nb:]
            return _conv3(up, ext, cw_ref, cb_ref, nb, 0)

        a = half(wua_ref, st0a_ref, st1a_ref, cwa_ref, cba_ref, cs0a_ref, cs1a_ref)
        g = half(wug_ref, st0g_ref, st1g_ref, cwg_ref, cbg_ref, cs0g_ref, cs1g_ref)
        act_scr[:, pl.ds(pl.multiple_of(j * tf, tf), tf)] = (_gelu(a) * g).astype(BF16)

    @pl.when(j >= nf)
    def _():
        cols = pl.ds(pl.multiple_of((j - nf) * tn, tn), tn)
        ff = jnp.dot(act_scr[...], wd_ref[...], preferred_element_type=F32)
        for l in range(seq):
            sl = slice(l * nb, (l + 1) * nb)
            x2_scr[sl, cols] = x1_ref[sl, :] + g2_ref[:, cols] * ff[sl, :]

    @pl.when(j == pl.num_programs(0) - 1)
    def _():
        y_ref[...] = _rms(x2_scr[...]) * fn_ref[...]


def _ffn_sample(h2, x1, mod, final_norm, st, w_up_t, conv_w, conv_b, w_down_t, *, nb, seq):
    r, d = x1.shape
    nf, tf = w_up_t.shape[0] // 2, w_up_t.shape[2]
    nd, dff, tn = w_down_t.shape
    up_j = lambda j: jnp.minimum(j, nf - 1)
    down_j = lambda j: jnp.maximum(j - nf, 0)
    st_spec = lambda row, half: pl.BlockSpec((nb, tf), lambda j: (0, (2 * row + half) * nf + up_j(j)))
    cs_spec = pl.BlockSpec((nb, tf), lambda j: (0, up_j(j)))
    cs_shape = jax.ShapeDtypeStruct((nb, dff), F32)
    kern = functools.partial(_ffn_sample_kernel, nb=nb, seq=seq, nf=nf, tf=tf, tn=tn)
    return pl.pallas_call(
        kern,
        grid=(nf + nd,),
        in_specs=[pl.BlockSpec((r, d), lambda j: (0, 0)),
                  pl.BlockSpec((r, tn), lambda j: (0, down_j(j))),
                  pl.BlockSpec((nb, d), lambda j: (0, 5)),
                  pl.BlockSpec((1, d), lambda j: (0, 0)),
                  st_spec(0, 0), st_spec(1, 0), st_spec(0, 1), st_spec(1, 1),
                  pl.BlockSpec((None, d, tf), lambda j: (up_j(j), 0, 0)),
                  pl.BlockSpec((None, d, tf), lambda j: (nf + up_j(j), 0, 0)),
                  pl.BlockSpec((3, tf), lambda j: (0, up_j(j))),
                  pl.BlockSpec((3, tf), lambda j: (0, nf + up_j(j))),
                  pl.BlockSpec((1, tf), lambda j: (0, up_j(j))),
                  pl.BlockSpec((1, tf), lambda j: (0, nf + up_j(j))),
                  pl.BlockSpec((None, dff, tn), lambda j: (down_j(j), 0, 0))],
        out_specs=[pl.BlockSpec((r, d), lambda j: (0, 0)), cs_spec, cs_spec, cs_spec, cs_spec],
        out_shape=[jax.ShapeDtypeStruct((r, d), F32), cs_shape, cs_shape, cs_shape, cs_shape],
        scratch_shapes=[pltpu.VMEM((r, dff), BF16), pltpu.VMEM((r, d), F32)],
        compiler_params=_cparams("arbitrary"),
        name="ffn_sample",
    )(h2, x1, mod, final_norm, st, st, st, st, w_up_t, w_up_t, conv_w, conv_w, conv_b, conv_b, w_down_t)


def _layer(xp, xs, cp, cs, st_gla, st_re, st_im, st_conv, w, final_norm):
    b, l, d = xp.shape
    nb, seq, _ = xs.shape
    _, h, dk, dv = st_gla.shape
    kw, gw = h * dk, h * dv
    g, p, ch = w['s5_b_re'].shape
    sw = g * ch
    rank = w['w_a2'].shape[0]
    dff = w['w_down'].shape[0]
    main_w = 2 * kw + 2 * gw
    row = lambda t: t.reshape(1, -1)

    w_in_bf = w['w_in'].astype(BF16)
    w_u_bf = w['w_in'][:, main_w + rank:].astype(BF16)
    w_a_bf = w['w_in'][:, main_w:main_w + rank].astype(BF16)
    w_glu_bf = w['w_glu'].astype(BF16)
    w_out_bf = w['w_out'].astype(BF16)
    w_up_t = w['w_up'].astype(BF16).reshape(d, 2 * dff // FFN_TF, FFN_TF).transpose(1, 0, 2)
    w_down_t = w['w_down'].astype(BF16).reshape(dff, d // FFN_TN, FFN_TN).transpose(1, 0, 2)

    n_c = b + nb
    n_c_pad = -(-n_c // 8) * 8
    c_all = jnp.concatenate([cp, cs, jnp.zeros((n_c_pad - n_c, d), F32)], axis=0)
    mod = _ada(c_all, w['w_ada'], row(w['b_ada']))
    mod_p = mod[:b].reshape(b, 1, 6 * d)
    mod_s = mod[b:n_c]

    xp2 = xp.reshape(b * l, d)
    xs2 = jnp.swapaxes(xs, 0, 1).reshape(seq * nb, d)
    tm_p = min(TM_PROMPT, l)
    tps = l // tm_p
    seq_of = lambda i: i // tps

    proj = functools.partial(_inproj, norm1=row(w['norm1']), w_in_bf=w_in_bf, w_u_bf=w_u_bf, w_a_bf=w_a_bf,
                             w_a2=w['w_a2'], b_a2=row(w['b_a2']), main_w=main_w)
    qkvg_p, u_p, la_p = proj(xp2, mod_p, seq_of, tm=tm_p, n_slabs=1, main_dtype=BF16)
    qkvg_s, u_s, la_s = proj(xs2, mod_s, None, tm=seq * nb, n_slabs=seq, main_dtype=F32)

    gn = row(w['gla_norm'])
    o_p, gla_p = _gla_prompt(qkvg_p, la_p, gn, b=b, l=l, h=h, dk=dk, dv=dv)
    o_s, gla_s = _gla_sample(qkvg_s, la_s, gn, st_gla, nb=nb, seq=seq, h=h, dk=dk, dv=dv)

    s5w = (w['s5_lam_re'], w['s5_lam_im'], w['s5_log_step'], w['s5_b_re'], w['s5_b_im'],
           w['s5_c_re'], w['s5_c_im'], w['s5_d'])
    cp_ = math.gcd(l, S5_CHUNK)
    n_chunk = l // cp_
    m_p, wst_p, v_p, lc_p = _s5_prep(*s5w, c_mat=cp_, c_real=cp_)
    y5_p, re_p, im_p = _s5_prompt(u_p, m_p, wst_p, v_p, lc_p, n_seq=b, n_chunk=n_chunk, c=cp_, ch=ch)
    re_p, im_p = re_p.reshape(b, g, p), im_p.reshape(b, g, p)

    m_s, wst_s, v_s, lc_s = _s5_prep(*s5w, c_mat=S5_CHUNK_SAMPLE, c_real=seq)
    y5_s, re_s, im_s = _s5_sample(u_s, st_re.reshape(nb, g * p), st_im.reshape(nb, g * p), m_s, wst_s, v_s, lc_s,
                                  seq=seq, nb=nb, ch=ch)
    re_s, im_s = re_s.reshape(nb, g, p), im_s.reshape(nb, g, p)

    mixer = functools.partial(_mix, norm2=row(w['norm2']), w_glu_bf=w_glu_bf, b_glu=row(w['b_glu']),
                              w_out_bf=w_out_bf)
    tm_mix = min(TM_MIX, l)
    x1_p, h2_p = mixer(o_p, y5_p, xp2, mod_p, lambda i: i // (l // tm_mix), tm=tm_mix, n_slabs=1)
    x1_s, h2_s = mixer(o_s, y5_s, xs2, mod_s, None, tm=seq * nb, n_slabs=seq)

    fn = row(final_norm)
    conv_w, conv_b = w['conv_w'], row(w['conv_b'])
    yp, cs_p = _ffn_prompt(h2_p, x1_p, mod_p, fn, w_up_t, conv_w, conv_b, w_down_t, b=b, l=l, tm=tm_p)
    conv_p = cs_p[tps - 1::tps].reshape(b, 2, 2 * dff)
    ys, cs0a, cs1a, cs0g, cs1g = _ffn_sample(h2_s, x1_s, mod_s, fn, st_conv.reshape(nb, 4 * dff), w_up_t,
                                             conv_w, conv_b, w_down_t, nb=nb, seq=seq)
    conv_s = jnp.concatenate([cs0a, cs0g, cs1a, cs1g], axis=1).reshape(nb, 2, 2 * dff)

    yp = yp.reshape(b, l, d)
    ys = jnp.swapaxes(ys.reshape(seq, nb, d), 0, 1)
    return yp, ys, (gla_p, re_p, im_p, conv_p), (gla_s, re_s, im_s, conv_s)


def kernel(x_prompt, x_sample, c_prompt, c_sample, state_gla, state_s5_re, state_s5_im, state_conv, w_ada, b_ada, norm1, w_in, w_a2, b_a2, gla_norm, s5_lam_re, s5_lam_im, s5_log_step, s5_b_re, s5_b_im, s5_c_re, s5_c_im, s5_d, w_glu, b_glu, w_out, norm2, w_up, conv_w, conv_b, w_down, final_norm):
    depth = w_ada.shape[0]
    assert depth == 1, "the final norm is fused into the last layer's FFN; only depth 1 is wired up"
    w = dict(w_ada=w_ada[0], b_ada=b_ada[0], norm1=norm1[0], w_in=w_in[0], w_a2=w_a2[0], b_a2=b_a2[0],
             gla_norm=gla_norm[0], s5_lam_re=s5_lam_re[0], s5_lam_im=s5_lam_im[0], s5_log_step=s5_log_step[0],
             s5_b_re=s5_b_re[0], s5_b_im=s5_b_im[0], s5_c_re=s5_c_re[0], s5_c_im=s5_c_im[0], s5_d=s5_d[0],
             w_glu=w_glu[0], b_glu=b_glu[0], w_out=w_out[0], norm2=norm2[0], w_up=w_up[0], conv_w=conv_w[0],
             conv_b=conv_b[0], w_down=w_down[0])
    yp, ys, sp, ss = _layer(x_prompt, x_sample, c_prompt, c_sample, state_gla[0], state_s5_re[0],
                            state_s5_im[0], state_conv[0], w, final_norm)
    stack = lambda t: t[None]
    return (yp, ys, stack(sp[0]), stack(sp[1]), stack(sp[2]), stack(sp[3]),
            stack(ss[0]), stack(ss[1]), stack(ss[2]), stack(ss[3]))
```

```python
import functools
import math

import jax
import jax.numpy as jnp
from jax import lax
from jax.experimental import pallas as pl
from jax.experimental.pallas import tpu as pltpu

F32 = jnp.float32
BF16 = jnp.bfloat16

NORM_EPS = 1e-6
GLA_TAU = 16.0
GLA_CHUNK = 64
GLA_SUB = 8
LOG2_E = 1.4426950408889634
GLA_SEQ_BLOCK = 512
LANES = 128
GLA_PAD = 128
S5_CHUNK = 16
S5_CHUNK_SAMPLE = 8
VMEM_LIMIT_BYTES = 56 * 1024 * 1024
TM_PROMPT = 512
TM_MIX = 512
TM_FFN = 1024
FFN_TF = 512
FFN_TN = 256
FFN_TY = 512


def _cparams(*sem):
    return pltpu.CompilerParams(dimension_semantics=sem, vmem_limit_bytes=VMEM_LIMIT_BYTES)


def _dot(a, b):
    return jnp.dot(a.astype(BF16), b.astype(BF16), preferred_element_type=F32)


def _dot_nt(a, b):
    return lax.dot_general(a.astype(BF16), b.astype(BF16), (((1,), (1,)), ((), ())),
                           preferred_element_type=F32)


def _dot_f32(a, b):
    return jnp.dot(a, b, preferred_element_type=F32, precision=lax.Precision.HIGHEST)


def _rms(x):
    return x * lax.rsqrt(jnp.mean(x * x, axis=-1, keepdims=True) + NORM_EPS)


def _gelu(x):
    return 0.5 * x * (1.0 + jnp.tanh(math.sqrt(2.0 / math.pi) * (x + 0.044715 * (x * x * x))))


def _sigmoid(x):
    return 1.0 / (1.0 + jnp.exp(-x))


def _ada_kernel(c_ref, w_ref, b_ref, o_ref):
    c = c_ref[...]
    o_ref[...] = _dot(c * _sigmoid(c), w_ref[...]) + b_ref[...]


def _ada(c_all, w_ada, b_ada, tn=1024):
    m, d = c_all.shape
    n = w_ada.shape[1]
    return pl.pallas_call(
        _ada_kernel,
        grid=(n // tn,),
        in_specs=[pl.BlockSpec((m, d), lambda j: (0, 0)),
                  pl.BlockSpec((d, tn), lambda j: (0, j)),
                  pl.BlockSpec((1, tn), lambda j: (0, j))],
        out_specs=pl.BlockSpec((m, tn), lambda j: (0, j)),
        out_shape=jax.ShapeDtypeStruct((m, n), F32),
        compiler_params=_cparams("arbitrary"),
        name="ada",
    )(c_all, w_ada, b_ada)


def _resident(shape, index_map):
    return pl.BlockSpec(shape, index_map, pipeline_mode=pl.Buffered(1))


def _inproj_kernel(x_ref, sh_ref, sc_ref, n1_ref, wm_ref, wu_ref, wa_ref, wa2_ref, ba2_ref,
                   qkvg_ref, u_ref, la_ref, h_scr, *, n_slabs, tn):
    rows = x_ref.shape[0] // n_slabs
    for l in range(n_slabs):
        sl = slice(l * rows, (l + 1) * rows)
        h = _rms(x_ref[sl, :]) * n1_ref[...] * (1.0 + sc_ref[...]) + sh_ref[...]
        h_scr[sl, :] = h.astype(BF16)
    a_lr = jnp.dot(h_scr[...], wa_ref[...], preferred_element_type=F32)
    z = _dot(a_lr, wa2_ref[...]) + ba2_ref[...]
    la_ref[...] = (jnp.minimum(z, 0.0) - jnp.log(1.0 + jnp.exp(-jnp.abs(z)))) / GLA_TAU
    for out_ref, w_ref in ((qkvg_ref, wm_ref), (u_ref, wu_ref)):
        for c0 in range(0, out_ref.shape[1], tn):
            part = jnp.dot(h_scr[...], w_ref[:, c0:c0 + tn], preferred_element_type=F32)
            out_ref[:, c0:c0 + tn] = part.astype(out_ref.dtype)


def _inproj(x, mod, mod_row, norm1, w_in_bf, w_u_bf, w_a_bf, w_a2, b_a2, *, tm, n_slabs, main_w, main_dtype,
            tn=512):
    r, d = x.shape
    sw = w_u_bf.shape[1]
    kw = w_a2.shape[1]
    rank = w_a_bf.shape[1]
    if mod.ndim == 3:
        mspec = lambda k: pl.BlockSpec((None, 1, d), lambda i: (mod_row(i), 0, k))
    else:
        mspec = lambda k: pl.BlockSpec((mod.shape[0], d), lambda i: (0, k))
    kern = functools.partial(_inproj_kernel, n_slabs=n_slabs, tn=tn)
    return pl.pallas_call(
        kern,
        grid=(r // tm,),
        in_specs=[pl.BlockSpec((tm, d), lambda i: (i, 0)),
                  mspec(0), mspec(1),
                  pl.BlockSpec((1, d), lambda i: (0, 0)),
                  _resident((d, main_w), lambda i: (0, 0)),
                  _resident((d, sw), lambda i: (0, 0)),
                  _resident((d, rank), lambda i: (0, 0)),
                  pl.BlockSpec((rank, kw), lambda i: (0, 0)),
                  pl.BlockSpec((1, kw), lambda i: (0, 0))],
        out_specs=[pl.BlockSpec((tm, main_w), lambda i: (i, 0)),
                   pl.BlockSpec((tm, sw), lambda i: (i, 0)),
                   pl.BlockSpec((tm, kw), lambda i: (i, 0))],
        out_shape=[jax.ShapeDtypeStruct((r, main_w), main_dtype),
                   jax.ShapeDtypeStruct((r, sw), F32),
                   jax.ShapeDtypeStruct((r, kw), F32)],
        scratch_shapes=[pltpu.VMEM((tm, d), BF16)],
        compiler_params=_cparams("arbitrary"),
        name="inproj",
    )(x, mod, mod, norm1, w_in_bf, w_u_bf, w_a_bf, w_a2, b_a2)


def _gla_chunk(q, k, v, ga, states, *, h, sub):
    c = q.shape[0]
    dk, dv = q.shape[1] // h, v.shape[1] // h
    rows = lax.broadcasted_iota(jnp.int32, (c, c), 0)
    cols = lax.broadcasted_iota(jnp.int32, (c, c), 1)
    sums, widths = [rows >= cols], []
    w = sub
    while w < c:
        sums.append(cols <= (rows // (2 * w)) * (2 * w) + (w - 1))
        widths.append(w)
        w *= 2
    sums.append(cols >= 0)
    pmat = jnp.concatenate(sums, axis=0).astype(F32).astype(BF16)
    ga_hi = ga.astype(BF16)
    ga_lo = (ga - ga_hi.astype(F32)).astype(BF16)
    tot = (jnp.dot(pmat, ga_hi, preferred_element_type=F32)
           + jnp.dot(pmat, ga_lo, preferred_element_type=F32))
    tot = tot * LOG2_E
    cum = tot[:c]
    last = tot[(len(sums) - 1) * c:]

    row_w = lax.broadcasted_iota(jnp.int32, q.shape, 0)
    lane_o = lax.broadcasted_iota(jnp.int32, (c, LANES), 1)
    by_off = [jnp.zeros((c, LANES), F32) for _ in range(h)]
    for off in range(sub):
        k_sh = k if off == 0 else pltpu.roll(k, off, axis=0)
        c_sh = cum if off == 0 else pltpu.roll(cum, off, axis=0)
        prod = q * k_sh * jnp.exp2(jnp.minimum(cum - c_sh, 0.0))
        for hh in range(h):
            col = jnp.sum(prod[:, hh * dk:(hh + 1) * dk], axis=-1, keepdims=True)
            by_off[hh] = jnp.where(lane_o == c - 1 - off, col, by_off[hh])
    same_sub = rows // sub == cols // sub
    att = []
    for hh in range(h):
        moved = pltpu.roll(by_off[hh], LANES - (c - 1), axis=1, stride=1, stride_axis=0)
        att.append(jnp.where(same_sub, moved[:, :c], 0.0))
    for lvl, w in enumerate(widths):
        ref = tot[(lvl + 1) * c:(lvl + 2) * c]
        odd = (row_w // w) % 2 == 1
        x = jnp.where(odd, q, k) * jnp.exp2(jnp.where(odd, cum - ref, ref - cum))
        pair = (rows // (2 * w) == cols // (2 * w)) & ((rows // w) % 2 == 1) & ((cols // w) % 2 == 0)
        for hh in range(h):
            xh = x[:, hh * dk:(hh + 1) * dk]
            att[hh] = att[hh] + jnp.where(pair, _dot_nt(xh, xh), 0.0)

    q_in = q * jnp.exp2(cum)
    k_out = k * jnp.exp2(last - cum)
    e_last = jnp.exp2(last[0:1])
    tail_row = lax.broadcasted_iota(jnp.int32, (GLA_PAD - c, dk), 0)
    outs, new_states = [], []
    for hh in range(h):
        ck, cv = slice(hh * dk, (hh + 1) * dk), slice(hh * dv, (hh + 1) * dv)
        outs.append(_dot(att[hh], v[:, cv]) + _dot(q_in[:, ck], states[hh]))
        tail = jnp.where(tail_row == 0, e_last[:, ck], 0.0)
        kt = jnp.concatenate([k_out[:, ck], tail], axis=0).T
        new_states.append(states[hh] * kt[:, c:c + 1] + _dot(kt[:, :c], v[:, cv]))
    return outs, new_states


def _gla_finish(o, g, gn):
    return _rms(o) * gn * (g * _sigmoid(g))


def _gla_prompt_kernel(q_ref, k_ref, v_ref, g_ref, la_ref, gn_ref, o_ref, s_ref, s_scr, *, h, dk, dv, chunk, sub,
                       scale):
    blk = pl.program_id(1)

    @pl.when(blk == 0)
    def _():
        s_scr[...] = jnp.zeros_like(s_scr)

    def body(n, carry):
        r = pl.ds(pl.multiple_of(n * chunk, chunk), chunk)
        outs, new_states = _gla_chunk(q_ref[r, :].astype(F32) * scale, k_ref[r, :].astype(F32),
                                      v_ref[r, :].astype(F32), la_ref[r, :],
                                      [s_scr[hh] for hh in range(h)], h=h, sub=sub)
        for hh in range(h):
            cv = slice(hh * dv, (hh + 1) * dv)
            s_scr[hh] = new_states[hh]
            o_ref[r, cv] = _gla_finish(outs[hh], g_ref[r, cv].astype(F32), gn_ref[...]).astype(o_ref.dtype)
        return carry

    lax.fori_loop(0, q_ref.shape[0] // chunk, body, 0)

    @pl.when(blk == pl.num_programs(1) - 1)
    def _():
        s_ref[...] = s_scr[...]


def _gla_prompt(qkvg, la, gla_norm, *, b, l, h, dk, dv):
    kw, gw = h * dk, h * dv
    chunk = math.gcd(l, GLA_CHUNK)
    sub = math.gcd(chunk, GLA_SUB)
    lb = math.gcd(l, GLA_SEQ_BLOCK)
    nlb = l // lb
    kern = functools.partial(_gla_prompt_kernel, h=h, dk=dk, dv=dv, chunk=chunk, sub=sub, scale=dk ** -0.5)
    return pl.pallas_call(
        kern,
        grid=(b, nlb),
        in_specs=[pl.BlockSpec((lb, kw), lambda i, j: (i * nlb + j, 0)),
                  pl.BlockSpec((lb, kw), lambda i, j: (i * nlb + j, 1)),
                  pl.BlockSpec((lb, gw), lambda i, j: (i * nlb + j, 2 * kw // gw)),
                  pl.BlockSpec((lb, gw), lambda i, j: (i * nlb + j, 2 * kw // gw + 1)),
                  pl.BlockSpec((lb, kw), lambda i, j: (i * nlb + j, 0)),
                  pl.BlockSpec((1, dv), lambda i, j: (0, 0))],
        out_specs=[pl.BlockSpec((lb, gw), lambda i, j: (i * nlb + j, 0)),
                   pl.BlockSpec((None, h, dk, dv), lambda i, j: (i, 0, 0, 0))],
        out_shape=[jax.ShapeDtypeStruct((b * l, gw), BF16),
                   jax.ShapeDtypeStruct((b, h, dk, dv), F32)],
        scratch_shapes=[pltpu.VMEM((h, dk, dv), F32)],
        compiler_params=_cparams("arbitrary", "arbitrary"),
        name="gla_prompt",
    )(qkvg, qkvg, qkvg, qkvg, la, gla_norm)


def _gla_sample_kernel(qkvg_ref, la_ref, gn_ref, s0_ref, o_ref, s_ref, *, nb, seq, h, dk, dv, pad, scale):
    bt = s0_ref.shape[0]
    kw, gw = h * dk, h * dv
    in_w = 2 * kw + 2 * gw
    i = pl.program_id(0)

    def rows_of(b, width, col):
        per_row = width // LANES
        return pl.ds(b * per_row + col // LANES, seq, stride=nb * per_row)

    def gather(ref, b, width, col, n_col):
        parts = [ref[rows_of(b, width, col + c), :] for c in range(0, n_col, LANES)]
        x = parts[0] if len(parts) == 1 else jnp.concatenate(parts, axis=1)
        return jnp.concatenate([x, jnp.zeros((pad - seq, n_col), F32)], axis=0)

    def body(bb, carry):
        b = i * bt + bb
        q = gather(qkvg_ref, b, in_w, 0, kw) * scale
        k = gather(qkvg_ref, b, in_w, kw, kw)
        v = gather(qkvg_ref, b, in_w, 2 * kw, gw)
        g = gather(qkvg_ref, b, in_w, 2 * kw + gw, gw)
        ga = gather(la_ref, b, kw, 0, kw)
        outs, new_states = _gla_chunk(q, k, v, ga, [s0_ref[bb, hh] for hh in range(h)], h=h, sub=pad)
        for hh in range(h):
            s_ref[bb, hh] = new_states[hh]
            res = _gla_finish(outs[hh], g[:, hh * dv:(hh + 1) * dv], gn_ref[...])
            for c in range(0, dv, LANES):
                o_ref[rows_of(b, gw, hh * dv + c), :] = res[:seq, c:c + LANES]
        return carry

    lax.fori_loop(0, bt, body, 0, unroll=4)


def _gla_sample(qkvg, la, gla_norm, s0, *, nb, seq, h, dk, dv, bt=8):
    gw = h * dv
    pad = 8
    assert dk % LANES == 0 and dv % LANES == 0
    qkvg, la = qkvg.reshape(-1, LANES), la.reshape(-1, LANES)
    o_rows = nb * seq * gw // LANES
    kern = functools.partial(_gla_sample_kernel, nb=nb, seq=seq, h=h, dk=dk, dv=dv, pad=pad, scale=dk ** -0.5)
    o, s_new = pl.pallas_call(
        kern,
        grid=(nb // bt,),
        in_specs=[pl.BlockSpec(qkvg.shape, lambda i: (0, 0)),
                  pl.BlockSpec(la.shape, lambda i: (0, 0)),
                  pl.BlockSpec((1, dv), lambda i: (0, 0)),
                  pl.BlockSpec((bt, h, dk, dv), lambda i: (i, 0, 0, 0))],
        out_specs=[pl.BlockSpec((o_rows, LANES), lambda i: (0, 0)),
                   pl.BlockSpec((bt, h, dk, dv), lambda i: (i, 0, 0, 0))],
        out_shape=[jax.ShapeDtypeStruct((o_rows, LANES), F32),
                   jax.ShapeDtypeStruct((nb, h, dk, dv), F32)],
        compiler_params=_cparams("arbitrary"),
        name="gla_sample",
    )(qkvg, la, gla_norm, s0)
    return o.reshape(nb * seq, gw), s_new


def _s5_prep_kernel(lamc_ref, lamr_ref, ls_ref, bt_re_ref, bt_im_ref, btile_re_ref, btile_im_ref,
                    ctile_re_ref, ctile_im_ref, d_ref, m_ref, wst_ref, v_ref, lc_ref, *, ch, c_mat, c_real):
    w = ch * c_mat
    p = lamc_ref.shape[1]
    lane_m = lax.broadcasted_iota(jnp.int32, (p, LANES), 1)
    m_f = jnp.where(lane_m <= c_mat, lane_m, 0).astype(F32)
    sel = lax.broadcasted_iota(jnp.int32, (LANES, w), 0)
    tau = lax.broadcasted_iota(jnp.int32, (LANES, w), 1) // ch
    spread_pw = (sel == tau).astype(F32)
    spread_pv = (sel == tau + 1).astype(F32)
    spread_ps = (sel == jnp.maximum(c_real - 1 - tau, 0)).astype(F32)
    rr = lax.broadcasted_iota(jnp.int32, (ch, w), 0)
    cc = lax.broadcasted_iota(jnp.int32, (ch, w), 1)
    spread_ch = (cc % ch == rr).astype(F32)

    def bbar_coef(lam_re, lam_im, lb_re, lb_im):
        den = lam_re * lam_re + lam_im * lam_im
        x, y = lb_re - 1.0, lb_im
        return (x * lam_re + y * lam_im) / den, (y * lam_re - x * lam_im) / den

    for gi in range(m_ref.shape[0]):
        dt = jnp.exp(ls_ref[gi])
        lam_re, lam_im = lamc_ref[gi, :, 0:1], lamc_ref[gi, :, 1:2]
        mag = jnp.exp(lam_re * dt * m_f)
        t_re, t_im = mag * jnp.cos(lam_im * dt * m_f), mag * jnp.sin(lam_im * dt * m_f)
        cf_re, cf_im = bbar_coef(lam_re, lam_im, t_re[:, 1:2], t_im[:, 1:2])

        ct_re, ct_im = _dot_f32(ctile_re_ref[gi], spread_ch), _dot_f32(ctile_im_ref[gi], spread_ch)
        bt_re, bt_im = _dot_f32(btile_re_ref[gi], spread_ch), _dot_f32(btile_im_ref[gi], spread_ch)
        pw_re, pw_im = _dot_f32(t_re, spread_pw), _dot_f32(t_im, spread_pw)
        cl_re = ct_re * pw_re - ct_im * pw_im
        cl_im = ct_re * pw_im + ct_im * pw_re
        pv_re, pv_im = _dot_f32(t_re, spread_pv), _dot_f32(t_im, spread_pv)
        v_ref[gi, 0:p, :] = (ct_re * pv_re - ct_im * pv_im).astype(v_ref.dtype)
        v_ref[gi, p:2 * p, :] = (-(ct_re * pv_im + ct_im * pv_re)).astype(v_ref.dtype)

        bb_re = cf_re * bt_re - cf_im * bt_im
        bb_im = cf_re * bt_im + cf_im * bt_re
        ps_re, ps_im = _dot_f32(t_re, spread_ps), _dot_f32(t_im, spread_ps)
        wst_ref[gi, 0:p, :] = (bb_re * ps_re - bb_im * ps_im).astype(wst_ref.dtype)
        wst_ref[gi, p:2 * p, :] = (bb_re * ps_im + bb_im * ps_re).astype(wst_ref.dtype)

        lr_re, lr_im = lamr_ref[gi, 0:1, :], lamr_ref[gi, 1:2, :]
        ar, tr = lr_re * dt, lr_im * dt
        ea = jnp.exp(ar)
        rf_re, rf_im = bbar_coef(lr_re, lr_im, ea * jnp.cos(tr), ea * jnp.sin(tr))
        bbt_re = rf_re * bt_re_ref[gi] - rf_im * bt_im_ref[gi]
        bbt_im = rf_re * bt_im_ref[gi] + rf_im * bt_re_ref[gi]
        kcat = _dot_f32(bbt_re, cl_re) - _dot_f32(bbt_im, cl_im)
        kcat = kcat + jnp.where(rr == cc, d_ref[gi], 0.0)
        for i in range(c_mat):
            shifted = kcat if i == 0 else pltpu.roll(kcat, ch * i, axis=1)
            m_ref[gi, i * ch:(i + 1) * ch, :] = jnp.where(cc >= ch * i, shifted, 0.0).astype(m_ref.dtype)

        ec = jnp.exp(ar * float(c_real))
        lc_ref[gi, 0:1, :] = ec * jnp.cos(tr * float(c_real))
        lc_ref[gi, 1:2, :] = ec * jnp.sin(tr * float(c_real))


def _s5_prep(lam_re, lam_im, log_step, b_re, b_im, c_re, c_im, d_skip, *, c_mat, c_real):
    g, p, ch = b_re.shape
    w = ch * c_mat
    gb = LANES // ch
    lam_col = jnp.stack([lam_re, lam_im], axis=-1)
    lam_row = jnp.stack([lam_re, lam_im], axis=1)
    ls = log_step.reshape(g, 1, 1)
    bt = lambda t: jnp.swapaxes(t, 1, 2)
    d_row = jnp.pad(d_skip.reshape(g, 1, ch), ((0, 0), (0, 0), (0, w - ch)))
    spec3 = lambda s: pl.BlockSpec((gb,) + s, lambda i: (i, 0, 0))
    kern = functools.partial(_s5_prep_kernel, ch=ch, c_mat=c_mat, c_real=c_real)
    return pl.pallas_call(
        kern,
        grid=(g // gb,),
        in_specs=[spec3((p, 2)), spec3((2, p)), spec3((1, 1)), spec3((ch, p)), spec3((ch, p)),
                  spec3((p, ch)), spec3((p, ch)), spec3((p, ch)), spec3((p, ch)), spec3((1, w))],
        out_specs=[spec3((w, w)), spec3((2 * p, w)), spec3((2 * p, w)), spec3((2, p))],
        out_shape=[jax.ShapeDtypeStruct((g, w, w), BF16), jax.ShapeDtypeStruct((g, 2 * p, w), BF16),
                   jax.ShapeDtypeStruct((g, 2 * p, w), BF16), jax.ShapeDtypeStruct((g, 2, p), F32)],
        compiler_params=_cparams("arbitrary"),
        name="s5_prep",
    )(lam_col, lam_row, ls, bt(b_re), bt(b_im), b_re, b_im, bt(c_re), bt(c_im), d_row)


def _cmul_rows(z, lr, li):
    p = lr.shape[1]
    coef_a = jnp.concatenate([lr, lr], axis=1)
    coef_b = jnp.concatenate([-li, li], axis=1)
    return z * coef_a + pltpu.roll(z, p, axis=1) * coef_b


def _regroup(parts, gi, ch):
    return jnp.concatenate([t[:, gi * ch:(gi + 1) * ch] for t in parts], axis=1)


def _s5_prompt_kernel(u_ref, m_ref, wst_ref, v_ref, lc_ref, y_ref, hre_ref, him_ref, *, ch, c, n_seq, n_chunk):
    gb = m_ref.shape[0]
    r = n_seq * n_chunk
    p = lc_ref.shape[2]
    toks = [pltpu.bitcast(u_ref[pl.ds(i, r, stride=c), :].astype(BF16), jnp.uint32) for i in range(c)]
    pos = lax.broadcasted_iota(jnp.int32, (r, 2 * p), 0) % n_chunk
    ys, h_re, h_im = [], [], []
    for gi in range(gb):
        x = pltpu.bitcast(_regroup(toks, gi, ch), BF16)
        z = lax.dot_general(x, wst_ref[gi], (((1,), (1,)), ((), ())), preferred_element_type=F32)
        lr, li = lc_ref[gi, 0:1, :], lc_ref[gi, 1:2, :]
        d = 1
        while d < n_chunk:
            zs = jnp.where(pos >= d, pltpu.roll(z, d, axis=0), 0.0)
            z = z + _cmul_rows(zs, lr, li)
            lr, li = lr * lr - li * li, 2.0 * lr * li
            d *= 2
        h_in = jnp.where(pos >= 1, pltpu.roll(z, 1, axis=0), 0.0)
        y = (jnp.dot(x, m_ref[gi], preferred_element_type=F32)
             + jnp.dot(h_in.astype(BF16), v_ref[gi], preferred_element_type=F32))
        ys.append(pltpu.bitcast(y.astype(BF16), jnp.uint32))
        hf = jnp.concatenate([z[(s + 1) * n_chunk - 1:(s + 1) * n_chunk] for s in range(n_seq)], axis=0)
        h_re.append(hf[:, :p])
        h_im.append(hf[:, p:])
    for j in range(c):
        y_ref[pl.ds(j, r, stride=c), :] = pltpu.bitcast(_regroup(ys, j, ch), BF16).astype(F32)
    hre_ref[...] = jnp.concatenate(h_re, axis=1)
    him_ref[...] = jnp.concatenate(h_im, axis=1)


def _s5_prompt(u, m, wst, v, lc, *, n_seq, n_chunk, c, ch):
    t, sw = u.shape
    g, w, _ = m.shape
    p2 = wst.shape[1]
    p = p2 // 2
    gb = LANES // ch
    spec3 = lambda s: pl.BlockSpec((gb,) + s, lambda i: (i, 0, 0))
    kern = functools.partial(_s5_prompt_kernel, ch=ch, c=c, n_seq=n_seq, n_chunk=n_chunk)
    return pl.pallas_call(
        kern,
        grid=(g // gb,),
        in_specs=[pl.BlockSpec((t, LANES), lambda i: (0, i)),
                  spec3((w, w)), spec3((p2, w)), spec3((p2, w)), spec3((2, p))],
        out_specs=[pl.BlockSpec((t, LANES), lambda i: (0, i)),
                   pl.BlockSpec((n_seq, gb * p), lambda i: (0, i)),
                   pl.BlockSpec((n_seq, gb * p), lambda i: (0, i))],
        out_shape=[jax.ShapeDtypeStruct((t, sw), F32), jax.ShapeDtypeStruct((n_seq, g * p), F32),
                   jax.ShapeDtypeStruct((n_seq, g * p), F32)],
        compiler_params=_cparams("arbitrary"),
        name="s5_prompt",
    )(u, m, wst, v, lc)


def _s5_sample_kernel(u_ref, sre_ref, sim_ref, m_ref, wst_ref, v_ref, lc_ref, y_ref, hre_ref, him_ref, *,
                      ch, seq, nb):
    gb = m_ref.shape[0]
    p = lc_ref.shape[2]
    w = m_ref.shape[1]
    toks = [u_ref[l * nb:(l + 1) * nb, :] for l in range(seq)]
    ys, h_re, h_im = [], [], []
    for gi in range(gb):
        x = jnp.concatenate([_regroup(toks, gi, ch), jnp.zeros((nb, w - seq * ch), F32)], axis=1).astype(BF16)
        h0 = jnp.concatenate([sre_ref[:, gi * p:(gi + 1) * p], sim_ref[:, gi * p:(gi + 1) * p]], axis=1)
        ys.append(jnp.dot(x, m_ref[gi], preferred_element_type=F32)
                  + jnp.dot(h0.astype(BF16), v_ref[gi], preferred_element_type=F32))
        hf = (_cmul_rows(h0, lc_ref[gi, 0:1, :], lc_ref[gi, 1:2, :])
              + lax.dot_general(x, wst_ref[gi], (((1,), (1,)), ((), ())), preferred_element_type=F32))
        h_re.append(hf[:, :p])
        h_im.append(hf[:, p:])
    for l in range(seq):
        y_ref[l * nb:(l + 1) * nb, :] = _regroup(ys, l, ch)
    hre_ref[...] = jnp.concatenate(h_re, axis=1)
    him_ref[...] = jnp.concatenate(h_im, axis=1)


def _s5_sample(u, st_re, st_im, m, wst, v, lc, *, seq, nb, ch):
    t, sw = u.shape
    g, w, _ = m.shape
    p2 = wst.shape[1]
    p = p2 // 2
    gb = LANES // ch
    spec3 = lambda s: pl.BlockSpec((gb,) + s, lambda i: (i, 0, 0))
    sspec = pl.BlockSpec((nb, gb * p), lambda i: (0, i))
    kern = functools.partial(_s5_sample_kernel, ch=ch, seq=seq, nb=nb)
    return pl.pallas_call(
        kern,
        grid=(g // gb,),
        in_specs=[pl.BlockSpec((t, LANES), lambda i: (0, i)), sspec, sspec,
                  spec3((w, w)), spec3((p2, w)), spec3((p2, w)), spec3((2, p))],
        out_specs=[pl.BlockSpec((t, LANES), lambda i: (0, i)), sspec, sspec],
        out_shape=[jax.ShapeDtypeStruct((t, sw), F32), jax.ShapeDtypeStruct((nb, g * p), F32),
                   jax.ShapeDtypeStruct((nb, g * p), F32)],
        compiler_params=_cparams("arbitrary"),
        name="s5_sample",
    )(u, st_re, st_im, m, wst, v, lc)


def _mix_kernel(o_ref, y_ref, x_ref, g1_ref, sh2_ref, sc2_ref, n2_ref, wglu_ref, bglu_ref, wo_ref,
                x1_ref, h2_ref, *, n_slabs, tn):
    rows = x_ref.shape[0] // n_slabs
    gw = o_ref.shape[1]
    yg = _gelu(y_ref[...])
    z = (yg * _sigmoid(_dot(yg, wglu_ref[...]) + bglu_ref[...])).astype(BF16)
    ob = o_ref[...].astype(BF16)
    for c0 in range(0, x_ref.shape[1], tn):
        cols = slice(c0, c0 + tn)
        mix = (jnp.dot(ob, wo_ref[0:gw, cols], preferred_element_type=F32)
               + jnp.dot(z, wo_ref[gw:, cols], preferred_element_type=F32))
        for l in range(n_slabs):
            sl = slice(l * rows, (l + 1) * rows)
            x1_ref[sl, cols] = x_ref[sl, cols] + g1_ref[:, cols] * mix[sl, :]
    for l in range(n_slabs):
        sl = slice(l * rows, (l + 1) * rows)
        h2 = _rms(x1_ref[sl, :]) * n2_ref[...] * (1.0 + sc2_ref[...]) + sh2_ref[...]
        h2_ref[sl, :] = h2.astype(BF16)


def _mix(o, y, x, mod, mod_row, norm2, w_glu_bf, b_glu, w_out_bf, *, tm, n_slabs, tn=512):
    r, d = x.shape
    gw = o.shape[1]
    sw = y.shape[1]
    if mod.ndim == 3:
        mspec = lambda k: pl.BlockSpec((None, 1, d), lambda i: (mod_row(i), 0, k))
    else:
        mspec = lambda k: pl.BlockSpec((mod.shape[0], d), lambda i: (0, k))
    kern = functools.partial(_mix_kernel, n_slabs=n_slabs, tn=tn)
    return pl.pallas_call(
        kern,
        grid=(r // tm,),
        in_specs=[pl.BlockSpec((tm, gw), lambda i: (i, 0)),
                  pl.BlockSpec((tm, sw), lambda i: (i, 0)),
                  pl.BlockSpec((tm, d), lambda i: (i, 0)),
                  mspec(2), mspec(3), mspec(4),
                  pl.BlockSpec((1, d), lambda i: (0, 0)),
                  _resident((sw, sw), lambda i: (0, 0)),
                  pl.BlockSpec((1, sw), lambda i: (0, 0)),
                  _resident((gw + sw, d), lambda i: (0, 0))],
        out_specs=[pl.BlockSpec((tm, d), lambda i: (i, 0)),
                   pl.BlockSpec((tm, d), lambda i: (i, 0))],
        out_shape=[jax.ShapeDtypeStruct((r, d), F32), jax.ShapeDtypeStruct((r, d), BF16)],
        compiler_params=_cparams("arbitrary"),
        name="mix",
    )(o, y, x, mod, mod, mod, norm2, w_glu_bf, b_glu, w_out_bf)


def _conv3(up, ext, cw_ref, cb_ref, off1, off2):
    n = up.shape[0]
    return (cw_ref[2:3, :] * up + cw_ref[1:2, :] * ext[off1:off1 + n] + cw_ref[0:1, :] * ext[off2:off2 + n]
            + cb_ref[...])


def _ffn_prompt_kernel(h_ref, x1_ref, g2_ref, fn_ref, wua_ref, wug_ref, cwa_ref, cwg_ref,
                       cba_ref, cbg_ref, wd_ref, y_ref, cs_ref, act_scr, x2_scr, carry_scr, inv_scr, *,
                       tiles_per_seq, nf, nd, tf, tn, ty, sub_tf):
    i, j = pl.program_id(0), pl.program_id(1)
    keep = carry_scr.shape[1]

    @pl.when((i == 0) & (j == 0))
    def _():
        carry_scr[...] = jnp.zeros_like(carry_scr)

    @pl.when(j < nf)
    def _():
        first = (i % tiles_per_seq) == 0
        h = h_ref[...]
        tm = h.shape[0]
        ups = [[jnp.dot(h, w_ref[:, c0:c0 + sub_tf], preferred_element_type=F32) for w_ref in (wua_ref, wug_ref)]
               for c0 in range(0, tf, sub_tf)]
        for ci, c0 in enumerate(range(0, tf, sub_tf)):
            cols = slice(c0, c0 + sub_tf)
            ccols = pl.ds(pl.multiple_of(j * tf + c0, sub_tf), sub_tf)

            def half(idx, w_ref, cw_ref, cb_ref):
                up = ups[ci][idx]
                prev = jnp.where(first, 0.0, carry_scr[idx, :, ccols])
                carry_scr[idx, :, ccols] = up[tm - keep:]
                ext = jnp.concatenate([prev, up], axis=0)
                conv = (cw_ref[2:3, cols] * up + cw_ref[1:2, cols] * ext[keep - 1:keep - 1 + tm]
                        + cw_ref[0:1, cols] * ext[keep - 2:keep - 2 + tm] + cb_ref[:, cols])
                return conv, up

            a, up_a = half(0, wua_ref, cwa_ref, cba_ref)
            g, up_g = half(1, wug_ref, cwg_ref, cbg_ref)
            for rr in range(2):
                row = tm - 2 + rr
                cs_ref[rr, :, cols] = jnp.concatenate([up_a[row:row + 1], up_g[row:row + 1]], axis=0)
            act_scr[:, ccols] = (_gelu(a) * g).astype(BF16)

    @pl.when((j >= nf) & (j < nf + nd))
    def _():
        cols = pl.ds(pl.multiple_of((j - nf) * tn, tn), tn)
        ff = jnp.dot(act_scr[...], wd_ref[...], preferred_element_type=F32)
        x2_scr[:, cols] = x1_ref[...] + g2_ref[:, cols] * ff

    @pl.when(j == nf + nd)
    def _():
        x2 = x2_scr[...]
        inv_scr[...] = jnp.broadcast_to(lax.rsqrt(jnp.mean(x2 * x2, axis=-1, keepdims=True) + NORM_EPS),
                                        inv_scr.shape)

    @pl.when(j >= nf + nd)
    def _():
        cols = pl.ds(pl.multiple_of((j - nf - nd) * ty, ty), ty)
        y_ref[...] = x2_scr[:, cols] * inv_scr[:, 0:1] * fn_ref[:, cols]


def _ffn_prompt(h2, x1, mod, final_norm, w_up_bf, conv_w, conv_b, w_down_bf, *, b, l, tm, tf=FFN_TF, tn=FFN_TN,
                ty=FFN_TY):
    r, d = x1.shape
    dff = w_down_bf.shape[0]
    nf, nd, ny = dff // tf, d // tn, d // ty
    tps = l // tm
    up_j = lambda j: jnp.minimum(j, nf - 1)
    down_j = lambda j: jnp.clip(j - nf, 0, nd - 1)
    out_j = lambda j: jnp.maximum(j - nf - nd, 0)
    kern = functools.partial(_ffn_prompt_kernel, tiles_per_seq=tps, nf=nf, nd=nd, tf=tf, tn=tn, ty=ty,
                             sub_tf=math.gcd(tf, 2 * LANES))
    return pl.pallas_call(
        kern,
        grid=(r // tm, nf + nd + ny),
        in_specs=[_resident((tm, d), lambda i, j: (i, 0)),
                  pl.BlockSpec((tm, tn), lambda i, j: (i, down_j(j))),
                  pl.BlockSpec((None, 1, d), lambda i, j: (i // tps, 0, 5)),
                  pl.BlockSpec((1, d), lambda i, j: (0, 0)),
                  pl.BlockSpec((d, tf), lambda i, j: (0, up_j(j))),
                  pl.BlockSpec((d, tf), lambda i, j: (0, nf + up_j(j))),
                  pl.BlockSpec((3, tf), lambda i, j: (0, up_j(j))),
                  pl.BlockSpec((3, tf), lambda i, j: (0, nf + up_j(j))),
                  pl.BlockSpec((1, tf), lambda i, j: (0, up_j(j))),
                  pl.BlockSpec((1, tf), lambda i, j: (0, nf + up_j(j))),
                  pl.BlockSpec((dff, tn), lambda i, j: (0, down_j(j)))],
        out_specs=[pl.BlockSpec((tm, ty), lambda i, j: (i, out_j(j))),
                   pl.BlockSpec((None, 2, 2, tf), lambda i, j: (i, 0, 0, up_j(j)))],
        out_shape=[jax.ShapeDtypeStruct((r, d), F32), jax.ShapeDtypeStruct((r // tm, 2, 2, dff), F32)],
        scratch_shapes=[pltpu.VMEM((tm, dff), BF16), pltpu.VMEM((tm, d), F32), pltpu.VMEM((2, 8, dff), F32),
                        pltpu.VMEM((tm, LANES), F32)],
        compiler_params=_cparams("arbitrary", "arbitrary"),
        name="ffn_prompt",
    )(h2, x1, mod, final_norm, w_up_bf, w_up_bf, conv_w, conv_w, conv_b, conv_b, w_down_bf)


def _ffn_sample_kernel(h_ref, x1_ref, g2_ref, fn_ref, st0a_ref, st1a_ref, st0g_ref, st1g_ref, wua_ref, wug_ref,
                       cwa_ref, cwg_ref, cba_ref, cbg_ref, wd_ref, y_ref, cs0a_ref, cs1a_ref, cs0g_ref, cs1g_ref,
                       act_scr, x2_scr, *, nb, seq, nf, tf, tn):
    j = pl.program_id(0)

    @pl.when(j < nf)
    def _():
        h = h_ref[...]

        def half(w_ref, st0_ref, st1_ref, cw_ref, cb_ref, cs0_ref, cs1_ref):
            up = jnp.dot(h, w_ref[...], preferred_element_type=F32)
            ext = jnp.concatenate([st0_ref[...], st1_ref[...], up], axis=0)
            cs0_ref[...] = up[(seq - 2) * nb:(seq - 1) * nb]
            cs1_ref[...] = up[(seq - 1) * nb:]
            return _conv3(up, ext, cw_ref, cb_ref, nb, 0)

        a = half(wua_ref, st0a_ref, st1a_ref, cwa_ref, cba_ref, cs0a_ref, cs1a_ref)
        g = half(wug_ref, st0g_ref, st1g_ref, cwg_ref, cbg_ref, cs0g_ref, cs1g_ref)
        act_scr[:, pl.ds(pl.multiple_of(j * tf, tf), tf)] = (_gelu(a) * g).astype(BF16)

    @pl.when(j >= nf)
    def _():
        cols = pl.ds(pl.multiple_of((j - nf) * tn, tn), tn)
        ff = jnp.dot(act_scr[...], wd_ref[...], preferred_element_type=F32)
        for l in range(seq):
            sl = slice(l * nb, (l + 1) * nb)
            x2_scr[sl, cols] = x1_ref[sl, :] + g2_ref[:, cols] * ff[sl, :]

    @pl.when(j == pl.num_programs(0) - 1)
    def _():
        y_ref[...] = _rms(x2_scr[...]) * fn_ref[...]


def _ffn_sample(h2, x1, mod, final_norm, st, w_up_bf, conv_w, conv_b, w_down_bf, *, nb, seq, tf=FFN_TF, tn=FFN_TN):
    r, d = x1.shape
    dff = w_down_bf.shape[0]
    nf, nd = dff // tf, d // tn
    up_j = lambda j: jnp.minimum(j, nf - 1)
    down_j = lambda j: jnp.maximum(j - nf, 0)
    st_spec = lambda row, half: pl.BlockSpec((nb, tf), lambda j: (0, (2 * row + half) * nf + up_j(j)))
    cs_spec = pl.BlockSpec((nb, tf), lambda j: (0, up_j(j)))
    cs_shape = jax.ShapeDtypeStruct((nb, dff), F32)
    kern = functools.partial(_ffn_sample_kernel, nb=nb, seq=seq, nf=nf, tf=tf, tn=tn)
    return pl.pallas_call(
        kern,
        grid=(nf + nd,),
        in_specs=[pl.BlockSpec((r, d), lambda j: (0, 0)),
                  pl.BlockSpec((r, tn), lambda j: (0, down_j(j))),
                  pl.BlockSpec((nb, d), lambda j: (0, 5)),
                  pl.BlockSpec((1, d), lambda j: (0, 0)),
                  st_spec(0, 0), st_spec(1, 0), st_spec(0, 1), st_spec(1, 1),
                  pl.BlockSpec((d, tf), lambda j: (0, up_j(j))),
                  pl.BlockSpec((d, tf), lambda j: (0, nf + up_j(j))),
                  pl.BlockSpec((3, tf), lambda j: (0, up_j(j))),
                  pl.BlockSpec((3, tf), lambda j: (0, nf + up_j(j))),
                  pl.BlockSpec((1, tf), lambda j: (0, up_j(j))),
                  pl.BlockSpec((1, tf), lambda j: (0, nf + up_j(j))),
                  pl.BlockSpec((dff, tn), lambda j: (0, down_j(j)))],
        out_specs=[pl.BlockSpec((r, d), lambda j: (0, 0)), cs_spec, cs_spec, cs_spec, cs_spec],
        out_shape=[jax.ShapeDtypeStruct((r, d), F32), cs_shape, cs_shape, cs_shape, cs_shape],
        scratch_shapes=[pltpu.VMEM((r, dff), BF16), pltpu.VMEM((r, d), F32)],
        compiler_params=_cparams("arbitrary"),
        name="ffn_sample",
    )(h2, x1, mod, final_norm, st, st, st, st, w_up_bf, w_up_bf, conv_w, conv_w, conv_b, conv_b, w_down_bf)


def _layer(xp, xs, cp, cs, st_gla, st_re, st_im, st_conv, w, final_norm):
    b, l, d = xp.shape
    nb, seq, _ = xs.shape
    _, h, dk, dv = st_gla.shape
    kw, gw = h * dk, h * dv
    g, p, ch = w['s5_b_re'].shape
    sw = g * ch
    rank = w['w_a2'].shape[0]
    dff = w['w_down'].shape[0]
    main_w = 2 * kw + 2 * gw
    row = lambda t: t.reshape(1, -1)

    w_in_bf = w['w_in'].astype(BF16)
    w_u_bf = w['w_in'][:, main_w + rank:].astype(BF16)
    w_a_bf = w['w_in'][:, main_w:main_w + rank].astype(BF16)
    w_glu_bf = w['w_glu'].astype(BF16)
    w_out_bf = w['w_out'].astype(BF16)
    w_up_bf = w['w_up'].astype(BF16)
    w_down_bf = w['w_down'].astype(BF16)

    n_c = b + nb
    n_c_pad = -(-n_c // 8) * 8
    c_all = jnp.concatenate([cp, cs, jnp.zeros((n_c_pad - n_c, d), F32)], axis=0)
    mod = _ada(c_all, w['w_ada'], row(w['b_ada']))
    mod_p = mod[:b].reshape(b, 1, 6 * d)
    mod_s = mod[b:n_c]

    xp2 = xp.reshape(b * l, d)
    xs2 = jnp.swapaxes(xs, 0, 1).reshape(seq * nb, d)
    tm_p = min(TM_PROMPT, l)
    tps = l // tm_p
    seq_of = lambda i: i // tps

    proj = functools.partial(_inproj, norm1=row(w['norm1']), w_in_bf=w_in_bf, w_u_bf=w_u_bf, w_a_bf=w_a_bf,
                             w_a2=w['w_a2'], b_a2=row(w['b_a2']), main_w=main_w)
    qkvg_p, u_p, la_p = proj(xp2, mod_p, seq_of, tm=tm_p, n_slabs=1, main_dtype=BF16)
    qkvg_s, u_s, la_s = proj(xs2, mod_s, None, tm=seq * nb, n_slabs=seq, main_dtype=F32)

    gn = row(w['gla_norm'])
    o_p, gla_p = _gla_prompt(qkvg_p, la_p, gn, b=b, l=l, h=h, dk=dk, dv=dv)
    o_s, gla_s = _gla_sample(qkvg_s, la_s, gn, st_gla, nb=nb, seq=seq, h=h, dk=dk, dv=dv)

    s5w = (w['s5_lam_re'], w['s5_lam_im'], w['s5_log_step'], w['s5_b_re'], w['s5_b_im'],
           w['s5_c_re'], w['s5_c_im'], w['s5_d'])
    cp_ = math.gcd(l, S5_CHUNK)
    n_chunk = l // cp_
    m_p, wst_p, v_p, lc_p = _s5_prep(*s5w, c_mat=cp_, c_real=cp_)
    y5_p, re_p, im_p = _s5_prompt(u_p, m_p, wst_p, v_p, lc_p, n_seq=b, n_chunk=n_chunk, c=cp_, ch=ch)
    re_p, im_p = re_p.reshape(b, g, p), im_p.reshape(b, g, p)

    m_s, wst_s, v_s, lc_s = _s5_prep(*s5w, c_mat=S5_CHUNK_SAMPLE, c_real=seq)
    y5_s, re_s, im_s = _s5_sample(u_s, st_re.reshape(nb, g * p), st_im.reshape(nb, g * p), m_s, wst_s, v_s, lc_s,
                                  seq=seq, nb=nb, ch=ch)
    re_s, im_s = re_s.reshape(nb, g, p), im_s.reshape(nb, g, p)

    mixer = functools.partial(_mix, norm2=row(w['norm2']), w_glu_bf=w_glu_bf, b_glu=row(w['b_glu']),
                              w_out_bf=w_out_bf)
    tm_mix = min(TM_MIX, l)
    x1_p, h2_p = mixer(o_p, y5_p, xp2, mod_p, lambda i: i // (l // tm_mix), tm=tm_mix, n_slabs=1)
    x1_s, h2_s = mixer(o_s, y5_s, xs2, mod_s, None, tm=seq * nb, n_slabs=seq)

    fn = row(final_norm)
    conv_w, conv_b = w['conv_w'], row(w['conv_b'])
    tm_f = min(TM_FFN, l)
    yp, cs_p = _ffn_prompt(h2_p, x1_p, mod_p, fn, w_up_bf, conv_w, conv_b, w_down_bf, b=b, l=l, tm=tm_f)
    conv_p = cs_p[l // tm_f - 1::l // tm_f].reshape(b, 2, 2 * dff)
    ys, cs0a, cs1a, cs0g, cs1g = _ffn_sample(h2_s, x1_s, mod_s, fn, st_conv.reshape(nb, 4 * dff), w_up_bf,
                                             conv_w, conv_b, w_down_bf, nb=nb, seq=seq)
    conv_s = jnp.concatenate([cs0a, cs0g, cs1a, cs1g], axis=1).reshape(nb, 2, 2 * dff)

    yp = yp.reshape(b, l, d)
    ys = jnp.swapaxes(ys.reshape(seq, nb, d), 0, 1)
    return yp, ys, (gla_p, re_p, im_p, conv_p), (gla_s, re_s, im_s, conv_s)


def kernel(x_prompt, x_sample, c_prompt, c_sample, state_gla, state_s5_re, state_s5_im, state_conv, w_ada, b_ada, norm1, w_in, w_a2, b_a2, gla_norm, s5_lam_re, s5_lam_im, s5_log_step, s5_b_re, s5_b_im, s5_c_re, s5_c_im, s5_d, w_glu, b_glu, w_out, norm2, w_up, conv_w, conv_b, w_down, final_norm):
    depth = w_ada.shape[0]
    assert depth == 1, "the final norm is fused into the last layer's FFN; only depth 1 is wired up"
    w = dict(w_ada=w_ada[0], b_ada=b_ada[0], norm1=norm1[0], w_in=w_in[0], w_a2=w_a2[0], b_a2=b_a2[0],
             gla_norm=gla_norm[0], s5_lam_re=s5_lam_re[0], s5_lam_im=s5_lam_im[0], s5_log_step=s5_log_step[0],
             s5_b_re=s5_b_re[0], s5_b_im=s5_b_im[0], s5_c_re=s5_c_re[0], s5_c_im=s5_c_im[0], s5_d=s5_d[0],
             w_glu=w_glu[0], b_glu=b_glu[0], w_out=w_out[0], norm2=norm2[0], w_up=w_up[0], conv_w=conv_w[0],
             conv_b=conv_b[0], w_down=w_down[0])
    yp, ys, sp, ss = _layer(x_prompt, x_sample, c_prompt, c_sample, state_gla[0], state_s5_re[0],
                            state_s5_im[0], state_conv[0], w, final_norm)
    stack = lambda t: t[None]
    return (yp, ys, stack(sp[0]), stack(sp[1]), stack(sp[2]), stack(sp[3]),
            stack(ss[0]), stack(ss[1]), stack(ss[2]), stack(ss[3]))
```

```python
import functools
import math

import jax
import jax.numpy as jnp
from jax import lax
from jax.experimental import pallas as pl
from jax.experimental.pallas import tpu as pltpu

F32 = jnp.float32
BF16 = jnp.bfloat16

NORM_EPS = 1e-6
GLA_TAU = 16.0
GLA_CHUNK = 128
GLA_SUB = 4
LOG2_E = 1.4426950408889634
GLA_SEQ_BLOCK = 512
LANES = 128
S5_CHUNK = 16
S5_CHUNK_SAMPLE = 8
VMEM_LIMIT_BYTES = 56 * 1024 * 1024
TM_PROMPT = 512
TM_MIX = 512
TM_FFN = 512
FFN_TF = 512
FFN_TN = 512
FFN_TY = 2048


def _cparams(*sem):
    return pltpu.CompilerParams(dimension_semantics=sem, vmem_limit_bytes=VMEM_LIMIT_BYTES)


def _dot(a, b):
    return jnp.dot(a.astype(BF16), b.astype(BF16), preferred_element_type=F32)


def _dot_nt(a, b):
    return lax.dot_general(a.astype(BF16), b.astype(BF16), (((1,), (1,)), ((), ())),
                           preferred_element_type=F32)


def _dot_f32(a, b):
    return jnp.dot(a, b, preferred_element_type=F32, precision=lax.Precision.HIGHEST)


def _spread_exact(xs, sel):
    rows = xs[0].shape[0]
    parts = []
    for x in xs:
        hi = x.astype(BF16)
        r1 = x - hi.astype(F32)
        mid = r1.astype(BF16)
        parts += [hi, mid, (r1 - mid.astype(F32)).astype(BF16)]
    y = jnp.dot(jnp.concatenate(parts, axis=0), sel.astype(BF16), preferred_element_type=F32)
    return [y[3 * i * rows:(3 * i + 1) * rows] + y[(3 * i + 1) * rows:(3 * i + 2) * rows]
            + y[(3 * i + 2) * rows:(3 * i + 3) * rows] for i in range(len(xs))]


def _rms(x):
    return x * lax.rsqrt(jnp.mean(x * x, axis=-1, keepdims=True) + NORM_EPS)


def _gelu(x):
    return 0.5 * x * (1.0 + jnp.tanh(math.sqrt(2.0 / math.pi) * (x + 0.044715 * (x * x * x))))


def _sigmoid(x):
    return 1.0 / (1.0 + jnp.exp(-x))


def _ada_kernel(c_ref, w_ref, b_ref, o_ref):
    c = c_ref[...]
    o_ref[...] = _dot(c * _sigmoid(c), w_ref[...]) + b_ref[...]


def _ada(c_all, w_ada, b_ada, tn=1024):
    m, d = c_all.shape
    n = w_ada.shape[1]
    return pl.pallas_call(
        _ada_kernel,
        grid=(n // tn,),
        in_specs=[pl.BlockSpec((m, d), lambda j: (0, 0)),
                  pl.BlockSpec((d, tn), lambda j: (0, j)),
                  pl.BlockSpec((1, tn), lambda j: (0, j))],
        out_specs=pl.BlockSpec((m, tn), lambda j: (0, j)),
        out_shape=jax.ShapeDtypeStruct((m, n), F32),
        compiler_params=_cparams("arbitrary"),
        name="ada",
    )(c_all, w_ada, b_ada)


def _resident(shape, index_map):
    return pl.BlockSpec(shape, index_map, pipeline_mode=pl.Buffered(1))


def _inproj_kernel(x_ref, sh_ref, sc_ref, n1_ref, wm_ref, wu_ref, wa_ref, wa2_ref, ba2_ref,
                   qkvg_ref, u_ref, la_ref, h_scr, *, n_slabs, tn):
    rows = x_ref.shape[0] // n_slabs
    for l in range(n_slabs):
        sl = slice(l * rows, (l + 1) * rows)
        h = _rms(x_ref[sl, :]) * n1_ref[...] * (1.0 + sc_ref[...]) + sh_ref[...]
        h_scr[sl, :] = h.astype(BF16)
    a_lr = jnp.dot(h_scr[...], wa_ref[...], preferred_element_type=F32)
    z = _dot(a_lr, wa2_ref[...]) + ba2_ref[...]
    la_ref[...] = (jnp.minimum(z, 0.0) - jnp.log(1.0 + jnp.exp(-jnp.abs(z)))) / GLA_TAU
    for out_ref, w_ref in ((qkvg_ref, wm_ref), (u_ref, wu_ref)):
        for c0 in range(0, out_ref.shape[1], tn):
            part = jnp.dot(h_scr[...], w_ref[:, c0:c0 + tn], preferred_element_type=F32)
            out_ref[:, c0:c0 + tn] = part.astype(out_ref.dtype)


def _inproj(x, mod, mod_row, norm1, w_in_bf, w_u_bf, w_a_bf, w_a2, b_a2, *, tm, n_slabs, main_w, main_dtype,
            tn=512):
    r, d = x.shape
    sw = w_u_bf.shape[1]
    kw = w_a2.shape[1]
    rank = w_a_bf.shape[1]
    if mod.ndim == 3:
        mspec = lambda k: pl.BlockSpec((None, 1, d), lambda i: (mod_row(i), 0, k))
    else:
        mspec = lambda k: pl.BlockSpec((mod.shape[0], d), lambda i: (0, k))
    kern = functools.partial(_inproj_kernel, n_slabs=n_slabs, tn=tn)
    return pl.pallas_call(
        kern,
        grid=(r // tm,),
        in_specs=[pl.BlockSpec((tm, d), lambda i: (i, 0)),
                  mspec(0), mspec(1),
                  pl.BlockSpec((1, d), lambda i: (0, 0)),
                  _resident((d, main_w), lambda i: (0, 0)),
                  _resident((d, sw), lambda i: (0, 0)),
                  _resident((d, rank), lambda i: (0, 0)),
                  pl.BlockSpec((rank, kw), lambda i: (0, 0)),
                  pl.BlockSpec((1, kw), lambda i: (0, 0))],
        out_specs=[pl.BlockSpec((tm, main_w), lambda i: (i, 0)),
                   pl.BlockSpec((tm, sw), lambda i: (i, 0)),
                   pl.BlockSpec((tm, kw), lambda i: (i, 0))],
        out_shape=[jax.ShapeDtypeStruct((r, main_w), main_dtype),
                   jax.ShapeDtypeStruct((r, sw), F32),
                   jax.ShapeDtypeStruct((r, kw), F32)],
        scratch_shapes=[pltpu.VMEM((tm, d), BF16)],
        compiler_params=_cparams("arbitrary"),
        name="inproj",
    )(x, mod, mod, norm1, w_in_bf, w_u_bf, w_a_bf, w_a2, b_a2)


def _gla_chunk(q, k, v, ga, states, *, h, sub):
    c = q.shape[0]
    dk, dv = q.shape[1] // h, v.shape[1] // h
    rows = lax.broadcasted_iota(jnp.int32, (c, c), 0)
    cols = lax.broadcasted_iota(jnp.int32, (c, c), 1)
    sums, widths = [rows >= cols], []
    w = sub
    while w < c:
        sums.append(cols <= (rows // (2 * w)) * (2 * w) + (w - 1))
        widths.append(w)
        w *= 2
    sums.append(cols >= 0)
    pmat = jnp.concatenate(sums, axis=0).astype(F32).astype(BF16)
    ga_hi = ga.astype(BF16)
    ga_lo = (ga - ga_hi.astype(F32)).astype(BF16)
    tot = (jnp.dot(pmat, ga_hi, preferred_element_type=F32)
           + jnp.dot(pmat, ga_lo, preferred_element_type=F32))
    tot = tot * LOG2_E
    cum = tot[:c]
    last = tot[(len(sums) - 1) * c:]

    row_w = lax.broadcasted_iota(jnp.int32, q.shape, 0)
    lane_o = lax.broadcasted_iota(jnp.int32, (c, LANES), 1)
    by_off = [jnp.zeros((c, LANES), F32) for _ in range(h)]
    for off in range(sub):
        k_sh = k if off == 0 else pltpu.roll(k, off, axis=0)
        c_sh = cum if off == 0 else pltpu.roll(cum, off, axis=0)
        prod = q * k_sh * jnp.exp2(jnp.minimum(cum - c_sh, 0.0))
        for hh in range(h):
            col = jnp.sum(prod[:, hh * dk:(hh + 1) * dk], axis=-1, keepdims=True)
            by_off[hh] = jnp.where(lane_o == c - 1 - off, col, by_off[hh])
    same_sub = rows // sub == cols // sub
    att = []
    for hh in range(h):
        moved = pltpu.roll(by_off[hh], LANES - (c - 1), axis=1, stride=1, stride_axis=0)
        att.append(jnp.where(same_sub, moved[:, :c], 0.0))
    for lvl, w in enumerate(widths):
        ref = tot[(lvl + 1) * c:(lvl + 2) * c]
        odd = (row_w // w) % 2 == 1
        x = jnp.where(odd, q, k) * jnp.exp2(jnp.where(odd, cum - ref, ref - cum))
        pair = (rows // (2 * w) == cols // (2 * w)) & ((rows // w) % 2 == 1) & ((cols // w) % 2 == 0)
        for hh in range(h):
            xh = x[:, hh * dk:(hh + 1) * dk]
            att[hh] = att[hh] + jnp.where(pair, _dot_nt(xh, xh), 0.0)

    q_in = q * jnp.exp2(cum)
    k_out = k * jnp.exp2(last - cum)
    e_last = jnp.exp2(last[0:1])
    pad_rows = -(-(c + 1) // LANES) * LANES
    tail_row = lax.broadcasted_iota(jnp.int32, (pad_rows - c, dk), 0)
    outs, new_states = [], []
    for hh in range(h):
        ck, cv = slice(hh * dk, (hh + 1) * dk), slice(hh * dv, (hh + 1) * dv)
        outs.append(_dot(att[hh], v[:, cv]) + _dot(q_in[:, ck], states[hh]))
        tail = jnp.where(tail_row == 0, e_last[:, ck], 0.0)
        kt = jnp.concatenate([k_out[:, ck], tail], axis=0).T
        new_states.append(states[hh] * kt[:, c:c + 1] + _dot(kt[:, :c], v[:, cv]))
    return outs, new_states


def _gla_finish(o, g, gn):
    return _rms(o) * gn * (g * _sigmoid(g))


def _gla_prompt_kernel(q_ref, k_ref, v_ref, g_ref, la_ref, gn_ref, o_ref, s_ref, s_scr, *, h, dk, dv, chunk, sub,
                       scale):
    blk = pl.program_id(1)

    @pl.when(blk == 0)
    def _():
        s_scr[...] = jnp.zeros_like(s_scr)

    def body(n, carry):
        r = pl.ds(pl.multiple_of(n * chunk, chunk), chunk)
        outs, new_states = _gla_chunk(q_ref[r, :].astype(F32) * scale, k_ref[r, :].astype(F32),
                                      v_ref[r, :].astype(F32), la_ref[r, :],
                                      [s_scr[hh] for hh in range(h)], h=h, sub=sub)
        for hh in range(h):
            cv = slice(hh * dv, (hh + 1) * dv)
            s_scr[hh] = new_states[hh]
            o_ref[r, cv] = _gla_finish(outs[hh], g_ref[r, cv].astype(F32), gn_ref[...]).astype(o_ref.dtype)
        return carry

    lax.fori_loop(0, q_ref.shape[0] // chunk, body, 0)

    @pl.when(blk == pl.num_programs(1) - 1)
    def _():
        s_ref[...] = s_scr[...]


def _gla_prompt(qkvg, la, gla_norm, *, b, l, h, dk, dv):
    kw, gw = h * dk, h * dv
    chunk = math.gcd(l, GLA_CHUNK)
    sub = math.gcd(chunk, GLA_SUB)
    lb = math.gcd(l, GLA_SEQ_BLOCK)
    nlb = l // lb
    kern = functools.partial(_gla_prompt_kernel, h=h, dk=dk, dv=dv, chunk=chunk, sub=sub, scale=dk ** -0.5)
    return pl.pallas_call(
        kern,
        grid=(b, nlb),
        in_specs=[pl.BlockSpec((lb, kw), lambda i, j: (i * nlb + j, 0)),
                  pl.BlockSpec((lb, kw), lambda i, j: (i * nlb + j, 1)),
                  pl.BlockSpec((lb, gw), lambda i, j: (i * nlb + j, 2 * kw // gw)),
                  pl.BlockSpec((lb, gw), lambda i, j: (i * nlb + j, 2 * kw // gw + 1)),
                  pl.BlockSpec((lb, kw), lambda i, j: (i * nlb + j, 0)),
                  pl.BlockSpec((1, dv), lambda i, j: (0, 0))],
        out_specs=[pl.BlockSpec((lb, gw), lambda i, j: (i * nlb + j, 0)),
                   pl.BlockSpec((None, h, dk, dv), lambda i, j: (i, 0, 0, 0))],
        out_shape=[jax.ShapeDtypeStruct((b * l, gw), BF16),
                   jax.ShapeDtypeStruct((b, h, dk, dv), F32)],
        scratch_shapes=[pltpu.VMEM((h, dk, dv), F32)],
        compiler_params=_cparams("arbitrary", "arbitrary"),
        name="gla_prompt",
    )(qkvg, qkvg, qkvg, qkvg, la, gla_norm)


def _gla_sample_kernel(qkvg_ref, la_ref, gn_ref, s0_ref, o_ref, s_ref, *, nb, seq, h, dk, dv, pad, scale):
    bt = s0_ref.shape[0]
    kw, gw = h * dk, h * dv
    in_w = 2 * kw + 2 * gw
    i = pl.program_id(0)

    def rows_of(b, width, col):
        per_row = width // LANES
        return pl.ds(b * per_row + col // LANES, seq, stride=nb * per_row)

    def gather(ref, b, width, col, n_col):
        parts = [ref[rows_of(b, width, col + c), :] for c in range(0, n_col, LANES)]
        x = parts[0] if len(parts) == 1 else jnp.concatenate(parts, axis=1)
        return jnp.concatenate([x, jnp.zeros((pad - seq, n_col), F32)], axis=0)

    def body(bb, carry):
        b = i * bt + bb
        q = gather(qkvg_ref, b, in_w, 0, kw) * scale
        k = gather(qkvg_ref, b, in_w, kw, kw)
        v = gather(qkvg_ref, b, in_w, 2 * kw, gw)
        g = gather(qkvg_ref, b, in_w, 2 * kw + gw, gw)
        ga = gather(la_ref, b, kw, 0, kw)
        outs, new_states = _gla_chunk(q, k, v, ga, [s0_ref[bb, hh] for hh in range(h)], h=h, sub=pad)
        for hh in range(h):
            s_ref[bb, hh] = new_states[hh]
            res = _gla_finish(outs[hh], g[:, hh * dv:(hh + 1) * dv], gn_ref[...])
            for c in range(0, dv, LANES):
                o_ref[rows_of(b, gw, hh * dv + c), :] = res[:seq, c:c + LANES]
        return carry

    lax.fori_loop(0, bt, body, 0, unroll=4)


def _gla_sample(qkvg, la, gla_norm, s0, *, nb, seq, h, dk, dv, bt=8):
    gw = h * dv
    pad = 8
    assert dk % LANES == 0 and dv % LANES == 0
    qkvg, la = qkvg.reshape(-1, LANES), la.reshape(-1, LANES)
    o_rows = nb * seq * gw // LANES
    kern = functools.partial(_gla_sample_kernel, nb=nb, seq=seq, h=h, dk=dk, dv=dv, pad=pad, scale=dk ** -0.5)
    o, s_new = pl.pallas_call(
        kern,
        grid=(nb // bt,),
        in_specs=[pl.BlockSpec(qkvg.shape, lambda i: (0, 0)),
                  pl.BlockSpec(la.shape, lambda i: (0, 0)),
                  pl.BlockSpec((1, dv), lambda i: (0, 0)),
                  pl.BlockSpec((bt, h, dk, dv), lambda i: (i, 0, 0, 0))],
        out_specs=[pl.BlockSpec((o_rows, LANES), lambda i: (0, 0)),
                   pl.BlockSpec((bt, h, dk, dv), lambda i: (i, 0, 0, 0))],
        out_shape=[jax.ShapeDtypeStruct((o_rows, LANES), F32),
                   jax.ShapeDtypeStruct((nb, h, dk, dv), F32)],
        compiler_params=_cparams("arbitrary"),
        name="gla_sample",
    )(qkvg, la, gla_norm, s0)
    return o.reshape(nb * seq, gw), s_new


def _s5_prep_kernel(lamc_ref, lamr_ref, ls_ref, bt_re_ref, bt_im_ref, btile_re_ref, btile_im_ref,
                    ctile_re_ref, ctile_im_ref, d_ref, m_ref, wst_ref, v_ref, lc_ref, *, ch, c_mat, c_real):
    w = ch * c_mat
    p = lamc_ref.shape[1]
    lane_m = lax.broadcasted_iota(jnp.int32, (p, LANES), 1)
    m_f = jnp.where(lane_m <= c_mat, lane_m, 0).astype(F32)
    sel = lax.broadcasted_iota(jnp.int32, (LANES, w), 0)
    tau = lax.broadcasted_iota(jnp.int32, (LANES, w), 1) // ch
    spread_pw = (sel == tau).astype(F32)
    spread_pv = (sel == tau + 1).astype(F32)
    spread_ps = (sel == jnp.maximum(c_real - 1 - tau, 0)).astype(F32)
    rr = lax.broadcasted_iota(jnp.int32, (ch, w), 0)
    cc = lax.broadcasted_iota(jnp.int32, (ch, w), 1)
    spread_ch = (cc % ch == rr).astype(F32)

    def bbar_coef(lam_re, lam_im, lb_re, lb_im):
        den = lam_re * lam_re + lam_im * lam_im
        x, y = lb_re - 1.0, lb_im
        return (x * lam_re + y * lam_im) / den, (y * lam_re - x * lam_im) / den

    gb = m_ref.shape[0]
    dts, tables = [], []
    for gi in range(gb):
        dt = jnp.exp(ls_ref[gi])
        mag = jnp.exp(lamc_ref[gi, :, 0:1] * dt * m_f)
        ang = lamc_ref[gi, :, 1:2] * dt * m_f
        dts.append(dt)
        tables += [mag * jnp.cos(ang), mag * jnp.sin(ang)]
    powers = _spread_exact(tables, jnp.concatenate([spread_pw, spread_pv, spread_ps], axis=1))
    tiles = _spread_exact([r[gi] for gi in range(gb)
                           for r in (ctile_re_ref, ctile_im_ref, btile_re_ref, btile_im_ref)], spread_ch)

    for gi in range(gb):
        dt = dts[gi]
        lam_re, lam_im = lamc_ref[gi, :, 0:1], lamc_ref[gi, :, 1:2]
        t_re, t_im = tables[2 * gi], tables[2 * gi + 1]
        cf_re, cf_im = bbar_coef(lam_re, lam_im, t_re[:, 1:2], t_im[:, 1:2])

        ct_re, ct_im, bt_re, bt_im = tiles[4 * gi:4 * gi + 4]
        pw_re, pw_im = powers[2 * gi][:, 0:w], powers[2 * gi + 1][:, 0:w]
        cl_re = ct_re * pw_re - ct_im * pw_im
        cl_im = ct_re * pw_im + ct_im * pw_re
        pv_re, pv_im = powers[2 * gi][:, w:2 * w], powers[2 * gi + 1][:, w:2 * w]
        v_ref[gi, 0:p, :] = (ct_re * pv_re - ct_im * pv_im).astype(v_ref.dtype)
        v_ref[gi, p:2 * p, :] = (-(ct_re * pv_im + ct_im * pv_re)).astype(v_ref.dtype)

        bb_re = cf_re * bt_re - cf_im * bt_im
        bb_im = cf_re * bt_im + cf_im * bt_re
        ps_re, ps_im = powers[2 * gi][:, 2 * w:], powers[2 * gi + 1][:, 2 * w:]
        wst_ref[gi, 0:p, :] = (bb_re * ps_re - bb_im * ps_im).astype(wst_ref.dtype)
        wst_ref[gi, p:2 * p, :] = (bb_re * ps_im + bb_im * ps_re).astype(wst_ref.dtype)

        lr_re, lr_im = lamr_ref[gi, 0:1, :], lamr_ref[gi, 1:2, :]
        ar, tr = lr_re * dt, lr_im * dt
        ea = jnp.exp(ar)
        rf_re, rf_im = bbar_coef(lr_re, lr_im, ea * jnp.cos(tr), ea * jnp.sin(tr))
        bbt_re = rf_re * bt_re_ref[gi] - rf_im * bt_im_ref[gi]
        bbt_im = rf_re * bt_im_ref[gi] + rf_im * bt_re_ref[gi]
        kcat = _dot_f32(bbt_re, cl_re) - _dot_f32(bbt_im, cl_im)
        kcat = kcat + jnp.where(rr == cc, d_ref[gi], 0.0)
        for i in range(c_mat):
            shifted = kcat if i == 0 else pltpu.roll(kcat, ch * i, axis=1)
            m_ref[gi, i * ch:(i + 1) * ch, :] = jnp.where(cc >= ch * i, shifted, 0.0).astype(m_ref.dtype)

        ec = jnp.exp(ar * float(c_real))
        lc_ref[gi, 0:1, :] = ec * jnp.cos(tr * float(c_real))
        lc_ref[gi, 1:2, :] = ec * jnp.sin(tr * float(c_real))


def _s5_prep(lam_re, lam_im, log_step, b_re, b_im, c_re, c_im, d_skip, *, c_mat, c_real):
    g, p, ch = b_re.shape
    w = ch * c_mat
    gb = LANES // ch
    lam_col = jnp.stack([lam_re, lam_im], axis=-1)
    lam_row = jnp.stack([lam_re, lam_im], axis=1)
    ls = log_step.reshape(g, 1, 1)
    bt = lambda t: jnp.swapaxes(t, 1, 2)
    d_row = jnp.pad(d_skip.reshape(g, 1, ch), ((0, 0), (0, 0), (0, w - ch)))
    spec3 = lambda s: pl.BlockSpec((gb,) + s, lambda i: (i, 0, 0))
    kern = functools.partial(_s5_prep_kernel, ch=ch, c_mat=c_mat, c_real=c_real)
    return pl.pallas_call(
        kern,
        grid=(g // gb,),
        in_specs=[spec3((p, 2)), spec3((2, p)), spec3((1, 1)), spec3((ch, p)), spec3((ch, p)),
                  spec3((p, ch)), spec3((p, ch)), spec3((p, ch)), spec3((p, ch)), spec3((1, w))],
        out_specs=[spec3((w, w)), spec3((2 * p, w)), spec3((2 * p, w)), spec3((2, p))],
        out_shape=[jax.ShapeDtypeStruct((g, w, w), BF16), jax.ShapeDtypeStruct((g, 2 * p, w), BF16),
                   jax.ShapeDtypeStruct((g, 2 * p, w), BF16), jax.ShapeDtypeStruct((g, 2, p), F32)],
        compiler_params=_cparams("arbitrary"),
        name="s5_prep",
    )(lam_col, lam_row, ls, bt(b_re), bt(b_im), b_re, b_im, bt(c_re), bt(c_im), d_row)


def _cmul_rows(z, lr, li):
    p = lr.shape[1]
    coef_a = jnp.concatenate([lr, lr], axis=1)
    coef_b = jnp.concatenate([-li, li], axis=1)
    return z * coef_a + pltpu.roll(z, p, axis=1) * coef_b


def _regroup(parts, gi, ch):
    return jnp.concatenate([t[:, gi * ch:(gi + 1) * ch] for t in parts], axis=1)


def _s5_prompt_kernel(u_ref, m_ref, wst_ref, v_ref, lc_ref, y_ref, hre_ref, him_ref, *, ch, c, n_seq, n_chunk):
    gb = m_ref.shape[0]
    r = n_seq * n_chunk
    p = lc_ref.shape[2]
    toks = [pltpu.bitcast(u_ref[pl.ds(i, r, stride=c), :].astype(BF16), jnp.uint32) for i in range(c)]
    pos = lax.broadcasted_iota(jnp.int32, (r, 2 * p), 0) % n_chunk
    ys, h_re, h_im = [], [], []
    for gi in range(gb):
        x = pltpu.bitcast(_regroup(toks, gi, ch), BF16)
        z = lax.dot_general(x, wst_ref[gi], (((1,), (1,)), ((), ())), preferred_element_type=F32)
        lr, li = lc_ref[gi, 0:1, :], lc_ref[gi, 1:2, :]
        d = 1
        while d < n_chunk:
            zs = jnp.where(pos >= d, pltpu.roll(z, d, axis=0), 0.0)
            z = z + _cmul_rows(zs, lr, li)
            lr, li = lr * lr - li * li, 2.0 * lr * li
            d *= 2
        h_in = jnp.where(pos >= 1, pltpu.roll(z, 1, axis=0), 0.0)
        y = (jnp.dot(x, m_ref[gi], preferred_element_type=F32)
             + jnp.dot(h_in.astype(BF16), v_ref[gi], preferred_element_type=F32))
        ys.append(pltpu.bitcast(y.astype(BF16), jnp.uint32))
        hf = jnp.concatenate([z[(s + 1) * n_chunk - 1:(s + 1) * n_chunk] for s in range(n_seq)], axis=0)
        h_re.append(hf[:, :p])
        h_im.append(hf[:, p:])
    for j in range(c):
        y_ref[pl.ds(j, r, stride=c), :] = pltpu.bitcast(_regroup(ys, j, ch), BF16).astype(F32)
    hre_ref[...] = jnp.concatenate(h_re, axis=1)
    him_ref[...] = jnp.concatenate(h_im, axis=1)


def _s5_prompt(u, m, wst, v, lc, *, n_seq, n_chunk, c, ch):
    t, sw = u.shape
    g, w, _ = m.shape
    p2 = wst.shape[1]
    p = p2 // 2
    gb = LANES // ch
    spec3 = lambda s: pl.BlockSpec((gb,) + s, lambda i: (i, 0, 0))
    kern = functools.partial(_s5_prompt_kernel, ch=ch, c=c, n_seq=n_seq, n_chunk=n_chunk)
    return pl.pallas_call(
        kern,
        grid=(g // gb,),
        in_specs=[pl.BlockSpec((t, LANES), lambda i: (0, i)),
                  spec3((w, w)), spec3((p2, w)), spec3((p2, w)), spec3((2, p))],
        out_specs=[pl.BlockSpec((t, LANES), lambda i: (0, i)),
                   pl.BlockSpec((n_seq, gb * p), lambda i: (0, i)),
                   pl.BlockSpec((n_seq, gb * p), lambda i: (0, i))],
        out_shape=[jax.ShapeDtypeStruct((t, sw), F32), jax.ShapeDtypeStruct((n_seq, g * p), F32),
                   jax.ShapeDtypeStruct((n_seq, g * p), F32)],
        compiler_params=_cparams("arbitrary"),
        name="s5_prompt",
    )(u, m, wst, v, lc)


def _s5_sample_kernel(u_ref, sre_ref, sim_ref, m_ref, wst_ref, v_ref, lc_ref, y_ref, hre_ref, him_ref, *,
                      ch, seq, nb):
    gb = m_ref.shape[0]
    p = lc_ref.shape[2]
    w = m_ref.shape[1]
    toks = [u_ref[l * nb:(l + 1) * nb, :] for l in range(seq)]
    ys, h_re, h_im = [], [], []
    for gi in range(gb):
        x = jnp.concatenate([_regroup(toks, gi, ch), jnp.zeros((nb, w - seq * ch), F32)], axis=1).astype(BF16)
        h0 = jnp.concatenate([sre_ref[:, gi * p:(gi + 1) * p], sim_ref[:, gi * p:(gi + 1) * p]], axis=1)
        ys.append(jnp.dot(x, m_ref[gi], preferred_element_type=F32)
                  + jnp.dot(h0.astype(BF16), v_ref[gi], preferred_element_type=F32))
        hf = (_cmul_rows(h0, lc_ref[gi, 0:1, :], lc_ref[gi, 1:2, :])
              + lax.dot_general(x, wst_ref[gi], (((1,), (1,)), ((), ())), preferred_element_type=F32))
        h_re.append(hf[:, :p])
        h_im.append(hf[:, p:])
    for l in range(seq):
        y_ref[l * nb:(l + 1) * nb, :] = _regroup(ys, l, ch)
    hre_ref[...] = jnp.concatenate(h_re, axis=1)
    him_ref[...] = jnp.concatenate(h_im, axis=1)


def _s5_sample(u, st_re, st_im, m, wst, v, lc, *, seq, nb, ch):
    t, sw = u.shape
    g, w, _ = m.shape
    p2 = wst.shape[1]
    p = p2 // 2
    gb = LANES // ch
    spec3 = lambda s: pl.BlockSpec((gb,) + s, lambda i: (i, 0, 0))
    sspec = pl.BlockSpec((nb, gb * p), lambda i: (0, i))
    kern = functools.partial(_s5_sample_kernel, ch=ch, seq=seq, nb=nb)
    return pl.pallas_call(
        kern,
        grid=(g // gb,),
        in_specs=[pl.BlockSpec((t, LANES), lambda i: (0, i)), sspec, sspec,
                  spec3((w, w)), spec3((p2, w)), spec3((p2, w)), spec3((2, p))],
        out_specs=[pl.BlockSpec((t, LANES), lambda i: (0, i)), sspec, sspec],
        out_shape=[jax.ShapeDtypeStruct((t, sw), F32), jax.ShapeDtypeStruct((nb, g * p), F32),
                   jax.ShapeDtypeStruct((nb, g * p), F32)],
        compiler_params=_cparams("arbitrary"),
        name="s5_sample",
    )(u, st_re, st_im, m, wst, v, lc)


def _mix_kernel(o_ref, y_ref, x_ref, g1_ref, sh2_ref, sc2_ref, n2_ref, wglu_ref, bglu_ref, wo_ref,
                x1_ref, h2_ref, *, n_slabs, tn):
    rows = x_ref.shape[0] // n_slabs
    gw = o_ref.shape[1]
    yg = _gelu(y_ref[...])
    z = (yg * _sigmoid(_dot(yg, wglu_ref[...]) + bglu_ref[...])).astype(BF16)
    ob = o_ref[...].astype(BF16)
    for c0 in range(0, x_ref.shape[1], tn):
        cols = slice(c0, c0 + tn)
        mix = (jnp.dot(ob, wo_ref[0:gw, cols], preferred_element_type=F32)
               + jnp.dot(z, wo_ref[gw:, cols], preferred_element_type=F32))
        for l in range(n_slabs):
            sl = slice(l * rows, (l + 1) * rows)
            x1_ref[sl, cols] = x_ref[sl, cols] + g1_ref[:, cols] * mix[sl, :]
    for l in range(n_slabs):
        sl = slice(l * rows, (l + 1) * rows)
        h2 = _rms(x1_ref[sl, :]) * n2_ref[...] * (1.0 + sc2_ref[...]) + sh2_ref[...]
        h2_ref[sl, :] = h2.astype(BF16)


def _mix(o, y, x, mod, mod_row, norm2, w_glu_bf, b_glu, w_out_bf, *, tm, n_slabs, tn=512):
    r, d = x.shape
    gw = o.shape[1]
    sw = y.shape[1]
    if mod.ndim == 3:
        mspec = lambda k: pl.BlockSpec((None, 1, d), lambda i: (mod_row(i), 0, k))
    else:
        mspec = lambda k: pl.BlockSpec((mod.shape[0], d), lambda i: (0, k))
    kern = functools.partial(_mix_kernel, n_slabs=n_slabs, tn=tn)
    return pl.pallas_call(
        kern,
        grid=(r // tm,),
        in_specs=[pl.BlockSpec((tm, gw), lambda i: (i, 0)),
                  pl.BlockSpec((tm, sw), lambda i: (i, 0)),
                  pl.BlockSpec((tm, d), lambda i: (i, 0)),
                  mspec(2), mspec(3), mspec(4),
                  pl.BlockSpec((1, d), lambda i: (0, 0)),
                  _resident((sw, sw), lambda i: (0, 0)),
                  pl.BlockSpec((1, sw), lambda i: (0, 0)),
                  _resident((gw + sw, d), lambda i: (0, 0))],
        out_specs=[pl.BlockSpec((tm, d), lambda i: (i, 0)),
                   pl.BlockSpec((tm, d), lambda i: (i, 0))],
        out_shape=[jax.ShapeDtypeStruct((r, d), F32), jax.ShapeDtypeStruct((r, d), BF16)],
        compiler_params=_cparams("arbitrary"),
        name="mix",
    )(o, y, x, mod, mod, mod, norm2, w_glu_bf, b_glu, w_out_bf)


def _conv3(up, ext, cw_ref, cb_ref, off1, off2):
    n = up.shape[0]
    return (cw_ref[2:3, :] * up + cw_ref[1:2, :] * ext[off1:off1 + n] + cw_ref[0:1, :] * ext[off2:off2 + n]
            + cb_ref[...])


def _ffn_prompt_kernel(h_ref, x1_ref, g2_ref, fn_ref, wua_ref, wug_ref, cwa_ref, cwg_ref,
                       cba_ref, cbg_ref, wd_ref, y_ref, cs_ref, act_scr, x2_scr, carry_scr, inv_scr, ext_scr, *,
                       tiles_per_seq, nf, nd, tf, tn, ty, sub_tf):
    i, j = pl.program_id(0), pl.program_id(1)
    keep = carry_scr.shape[1]

    @pl.when((i == 0) & (j == 0))
    def _():
        carry_scr[...] = jnp.zeros_like(carry_scr)

    @pl.when(j < nf)
    def _():
        first = (i % tiles_per_seq) == 0
        h = h_ref[...]
        tm = h.shape[0]
        for ci, c0 in enumerate(range(0, tf, sub_tf)):
            cols = slice(c0, c0 + sub_tf)
            ccols = pl.ds(pl.multiple_of(j * tf + c0, sub_tf), sub_tf)

            def half(idx, w_ref, cw_ref, cb_ref):
                ext = ext_scr.at[ci, idx]
                up = jnp.dot(h, w_ref[:, cols], preferred_element_type=F32)
                ext[0:keep, :] = jnp.where(first, 0.0, carry_scr[idx, :, ccols])
                ext[keep:, :] = up
                carry_scr[idx, :, ccols] = up[tm - keep:]
                conv = (cw_ref[2:3, cols] * up + cw_ref[1:2, cols] * ext[keep - 1:keep - 1 + tm, :]
                        + cw_ref[0:1, cols] * ext[keep - 2:keep - 2 + tm, :] + cb_ref[:, cols])
                return conv, up

            a, up_a = half(0, wua_ref, cwa_ref, cba_ref)
            g, up_g = half(1, wug_ref, cwg_ref, cbg_ref)
            for rr in range(2):
                row = tm - 2 + rr
                cs_ref[rr, :, cols] = jnp.concatenate([up_a[row:row + 1], up_g[row:row + 1]], axis=0)
            act_scr[:, ccols] = (_gelu(a) * g).astype(BF16)

    @pl.when((j >= nf) & (j < nf + nd))
    def _():
        cols = pl.ds(pl.multiple_of((j - nf) * tn, tn), tn)
        ff = jnp.dot(act_scr[...], wd_ref[...], preferred_element_type=F32)
        x2_scr[:, cols] = x1_ref[...] + g2_ref[:, cols] * ff

    @pl.when(j == nf + nd)
    def _():
        x2 = x2_scr[...]
        inv_scr[...] = jnp.broadcast_to(lax.rsqrt(jnp.mean(x2 * x2, axis=-1, keepdims=True) + NORM_EPS),
                                        inv_scr.shape)

    @pl.when(j >= nf + nd)
    def _():
        cols = pl.ds(pl.multiple_of((j - nf - nd) * ty, ty), ty)
        y_ref[...] = x2_scr[:, cols] * inv_scr[:, 0:1] * fn_ref[:, cols]


def _ffn_prompt(h2, x1, mod, final_norm, w_up_bf, conv_w, conv_b, w_down_bf, *, b, l, tm, tf=FFN_TF, tn=FFN_TN,
                ty=FFN_TY):
    r, d = x1.shape
    dff = w_down_bf.shape[0]
    nf, nd, ny = dff // tf, d // tn, d // ty
    tps = l // tm
    up_j = lambda j: jnp.minimum(j, nf - 1)
    down_j = lambda j: jnp.clip(j - nf, 0, nd - 1)
    out_j = lambda j: jnp.maximum(j - nf - nd, 0)
    sub_tf = math.gcd(tf, 2 * LANES)
    kern = functools.partial(_ffn_prompt_kernel, tiles_per_seq=tps, nf=nf, nd=nd, tf=tf, tn=tn, ty=ty,
                             sub_tf=sub_tf)
    return pl.pallas_call(
        kern,
        grid=(r // tm, nf + nd + ny),
        in_specs=[_resident((tm, d), lambda i, j: (i, 0)),
                  pl.BlockSpec((tm, tn), lambda i, j: (i, down_j(j))),
                  pl.BlockSpec((None, 1, d), lambda i, j: (i // tps, 0, 5)),
                  pl.BlockSpec((1, d), lambda i, j: (0, 0)),
                  pl.BlockSpec((d, tf), lambda i, j: (0, up_j(j))),
                  pl.BlockSpec((d, tf), lambda i, j: (0, nf + up_j(j))),
                  pl.BlockSpec((3, tf), lambda i, j: (0, up_j(j))),
                  pl.BlockSpec((3, tf), lambda i, j: (0, nf + up_j(j))),
                  pl.BlockSpec((1, tf), lambda i, j: (0, up_j(j))),
                  pl.BlockSpec((1, tf), lambda i, j: (0, nf + up_j(j))),
                  pl.BlockSpec((dff, tn), lambda i, j: (0, down_j(j)))],
        out_specs=[pl.BlockSpec((tm, ty), lambda i, j: (i, out_j(j))),
                   pl.BlockSpec((None, 2, 2, tf), lambda i, j: (i, 0, 0, up_j(j)))],
        out_shape=[jax.ShapeDtypeStruct((r, d), F32), jax.ShapeDtypeStruct((r // tm, 2, 2, dff), F32)],
        scratch_shapes=[pltpu.VMEM((tm, dff), BF16), pltpu.VMEM((tm, d), F32), pltpu.VMEM((2, 8, dff), F32),
                        pltpu.VMEM((tm, LANES), F32), pltpu.VMEM((tf // sub_tf, 2, 8 + tm, sub_tf), F32)],
        compiler_params=_cparams("arbitrary", "arbitrary"),
        name="ffn_prompt",
    )(h2, x1, mod, final_norm, w_up_bf, w_up_bf, conv_w, conv_w, conv_b, conv_b, w_down_bf)


def _ffn_sample_kernel(h_ref, x1_ref, g2_ref, fn_ref, st0a_ref, st1a_ref, st0g_ref, st1g_ref, wua_ref, wug_ref,
                       cwa_ref, cwg_ref, cba_ref, cbg_ref, wd_ref, y_ref, cs0a_ref, cs1a_ref, cs0g_ref, cs1g_ref,
                       act_scr, x2_scr, *, nb, seq, nf, tf, tn):
    j = pl.program_id(0)

    @pl.when(j < nf)
    def _():
        h = h_ref[...]

        def half(w_ref, st0_ref, st1_ref, cw_ref, cb_ref, cs0_ref, cs1_ref):
            up = jnp.dot(h, w_ref[...], preferred_element_type=F32)
            ext = jnp.concatenate([st0_ref[...], st1_ref[...], up], axis=0)
            cs0_ref[...] = up[(seq - 2) * nb:(seq - 1) * nb]
            cs1_ref[...] = up[(seq - 1) * nb:]
            return _conv3(up, ext, cw_ref, cb_ref, nb, 0)

        a = half(wua_ref, st0a_ref, st1a_ref, cwa_ref, cba_ref, cs0a_ref, cs1a_ref)
        g = half(wug_ref, st0g_ref, st1g_ref, cwg_ref, cbg_ref, cs0g_ref, cs1g_ref)
        act_scr[:, pl.ds(pl.multiple_of(j * tf, tf), tf)] = (_gelu(a) * g).astype(BF16)

    @pl.when(j >= nf)
    def _():
        cols = pl.ds(pl.multiple_of((j - nf) * tn, tn), tn)
        ff = jnp.dot(act_scr[...], wd_ref[...], preferred_element_type=F32)
        for l in range(seq):
            sl = slice(l * nb, (l + 1) * nb)
            x2_scr[sl, cols] = x1_ref[sl, :] + g2_ref[:, cols] * ff[sl, :]

    @pl.when(j == pl.num_programs(0) - 1)
    def _():
        y_ref[...] = _rms(x2_scr[...]) * fn_ref[...]


def _ffn_sample(h2, x1, mod, final_norm, st, w_up_bf, conv_w, conv_b, w_down_bf, *, nb, seq, tf=FFN_TF, tn=FFN_TN):
    r, d = x1.shape
    dff = w_down_bf.shape[0]
    nf, nd = dff // tf, d // tn
    up_j = lambda j: jnp.minimum(j, nf - 1)
    down_j = lambda j: jnp.maximum(j - nf, 0)
    st_spec = lambda row, half: pl.BlockSpec((nb, tf), lambda j: (0, (2 * row + half) * nf + up_j(j)))
    cs_spec = pl.BlockSpec((nb, tf), lambda j: (0, up_j(j)))
    cs_shape = jax.ShapeDtypeStruct((nb, dff), F32)
    kern = functools.partial(_ffn_sample_kernel, nb=nb, seq=seq, nf=nf, tf=tf, tn=tn)
    return pl.pallas_call(
        kern,
        grid=(nf + nd,),
        in_specs=[pl.BlockSpec((r, d), lambda j: (0, 0)),
                  pl.BlockSpec((r, tn), lambda j: (0, down_j(j))),
                  pl.BlockSpec((nb, d), lambda j: (0, 5)),
                  pl.BlockSpec((1, d), lambda j: (0, 0)),
                  st_spec(0, 0), st_spec(1, 0), st_spec(0, 1), st_spec(1, 1),
                  pl.BlockSpec((d, tf), lambda j: (0, up_j(j))),
                  pl.BlockSpec((d, tf), lambda j: (0, nf + up_j(j))),
                  pl.BlockSpec((3, tf), lambda j: (0, up_j(j))),
                  pl.BlockSpec((3, tf), lambda j: (0, nf + up_j(j))),
                  pl.BlockSpec((1, tf), lambda j: (0, up_j(j))),
                  pl.BlockSpec((1, tf), lambda j: (0, nf + up_j(j))),
                  pl.BlockSpec((dff, tn), lambda j: (0, down_j(j)))],
        out_specs=[pl.BlockSpec((r, d), lambda j: (0, 0)), cs_spec, cs_spec, cs_spec, cs_spec],
        out_shape=[jax.ShapeDtypeStruct((r, d), F32), cs_shape, cs_shape, cs_shape, cs_shape],
        scratch_shapes=[pltpu.VMEM((r, dff), BF16), pltpu.VMEM((r, d), F32)],
        compiler_params=_cparams("arbitrary"),
        name="ffn_sample",
    )(h2, x1, mod, final_norm, st, st, st, st, w_up_bf, w_up_bf, conv_w, conv_w, conv_b, conv_b, w_down_bf)


def _layer(xp, xs, cp, cs, st_gla, st_re, st_im, st_conv, w, final_norm):
    b, l, d = xp.shape
    nb, seq, _ = xs.shape
    _, h, dk, dv = st_gla.shape
    kw, gw = h * dk, h * dv
    g, p, ch = w['s5_b_re'].shape
    sw = g * ch
    rank = w['w_a2'].shape[0]
    dff = w['w_down'].shape[0]
    main_w = 2 * kw + 2 * gw
    row = lambda t: t.reshape(1, -1)

    w_in_bf = w['w_in'].astype(BF16)
    w_u_bf = w['w_in'][:, main_w + rank:].astype(BF16)
    w_a_bf = w['w_in'][:, main_w:main_w + rank].astype(BF16)
    w_glu_bf = w['w_glu'].astype(BF16)
    w_out_bf = w['w_out'].astype(BF16)
    w_up_bf = w['w_up'].astype(BF16)
    w_down_bf = w['w_down'].astype(BF16)

    n_c = b + nb
    n_c_pad = -(-n_c // 8) * 8
    c_all = jnp.concatenate([cp, cs, jnp.zeros((n_c_pad - n_c, d), F32)], axis=0)
    mod = _ada(c_all, w['w_ada'], row(w['b_ada']))
    mod_p = mod[:b].reshape(b, 1, 6 * d)
    mod_s = mod[b:n_c]

    xp2 = xp.reshape(b * l, d)
    xs2 = jnp.swapaxes(xs, 0, 1).reshape(seq * nb, d)
    tm_p = min(TM_PROMPT, l)
    tps = l // tm_p
    seq_of = lambda i: i // tps

    proj = functools.partial(_inproj, norm1=row(w['norm1']), w_in_bf=w_in_bf, w_u_bf=w_u_bf, w_a_bf=w_a_bf,
                             w_a2=w['w_a2'], b_a2=row(w['b_a2']), main_w=main_w)
    qkvg_p, u_p, la_p = proj(xp2, mod_p, seq_of, tm=tm_p, n_slabs=1, main_dtype=BF16)
    qkvg_s, u_s, la_s = proj(xs2, mod_s, None, tm=seq * nb, n_slabs=seq, main_dtype=F32)

    gn = row(w['gla_norm'])
    o_p, gla_p = _gla_prompt(qkvg_p, la_p, gn, b=b, l=l, h=h, dk=dk, dv=dv)
    o_s, gla_s = _gla_sample(qkvg_s, la_s, gn, st_gla, nb=nb, seq=seq, h=h, dk=dk, dv=dv)

    s5w = (w['s5_lam_re'], w['s5_lam_im'], w['s5_log_step'], w['s5_b_re'], w['s5_b_im'],
           w['s5_c_re'], w['s5_c_im'], w['s5_d'])
    cp_ = math.gcd(l, S5_CHUNK)
    n_chunk = l // cp_
    m_p, wst_p, v_p, lc_p = _s5_prep(*s5w, c_mat=cp_, c_real=cp_)
    y5_p, re_p, im_p = _s5_prompt(u_p, m_p, wst_p, v_p, lc_p, n_seq=b, n_chunk=n_chunk, c=cp_, ch=ch)
    re_p, im_p = re_p.reshape(b, g, p), im_p.reshape(b, g, p)

    m_s, wst_s, v_s, lc_s = _s5_prep(*s5w, c_mat=S5_CHUNK_SAMPLE, c_real=seq)
    y5_s, re_s, im_s = _s5_sample(u_s, st_re.reshape(nb, g * p), st_im.reshape(nb, g * p), m_s, wst_s, v_s, lc_s,
                                  seq=seq, nb=nb, ch=ch)
    re_s, im_s = re_s.reshape(nb, g, p), im_s.reshape(nb, g, p)

    mixer = functools.partial(_mix, norm2=row(w['norm2']), w_glu_bf=w_glu_bf, b_glu=row(w['b_glu']),
                              w_out_bf=w_out_bf)
    tm_mix = min(TM_MIX, l)
    x1_p, h2_p = mixer(o_p, y5_p, xp2, mod_p, lambda i: i // (l // tm_mix), tm=tm_mix, n_slabs=1)
    x1_s, h2_s = mixer(o_s, y5_s, xs2, mod_s, None, tm=seq * nb, n_slabs=seq)

    fn = row(final_norm)
    conv_w, conv_b = w['conv_w'], row(w['conv_b'])
    tm_f = min(TM_FFN, l)
    yp, cs_p = _ffn_prompt(h2_p, x1_p, mod_p, fn, w_up_bf, conv_w, conv_b, w_down_bf, b=b, l=l, tm=tm_f)
    conv_p = cs_p[l // tm_f - 1::l // tm_f].reshape(b, 2, 2 * dff)
    ys, cs0a, cs1a, cs0g, cs1g = _ffn_sample(h2_s, x1_s, mod_s, fn, st_conv.reshape(nb, 4 * dff), w_up_bf,
                                             conv_w, conv_b, w_down_bf, nb=nb, seq=seq)
    conv_s = jnp.concatenate([cs0a, cs0g, cs1a, cs1g], axis=1).reshape(nb, 2, 2 * dff)

    yp = yp.reshape(b, l, d)
    ys = jnp.swapaxes(ys.reshape(seq, nb, d), 0, 1)
    return yp, ys, (gla_p, re_p, im_p, conv_p), (gla_s, re_s, im_s, conv_s)


def kernel(x_prompt, x_sample, c_prompt, c_sample, state_gla, state_s5_re, state_s5_im, state_conv, w_ada, b_ada, norm1, w_in, w_a2, b_a2, gla_norm, s5_lam_re, s5_lam_im, s5_log_step, s5_b_re, s5_b_im, s5_c_re, s5_c_im, s5_d, w_glu, b_glu, w_out, norm2, w_up, conv_w, conv_b, w_down, final_norm):
    depth = w_ada.shape[0]
    assert depth == 1, "the final norm is fused into the last layer's FFN; only depth 1 is wired up"
    w = dict(w_ada=w_ada[0], b_ada=b_ada[0], norm1=norm1[0], w_in=w_in[0], w_a2=w_a2[0], b_a2=b_a2[0],
             gla_norm=gla_norm[0], s5_lam_re=s5_lam_re[0], s5_lam_im=s5_lam_im[0], s5_log_step=s5_log_step[0],
             s5_b_re=s5_b_re[0], s5_b_im=s5_b_im[0], s5_c_re=s5_c_re[0], s5_c_im=s5_c_im[0], s5_d=s5_d[0],
             w_glu=w_glu[0], b_glu=b_glu[0], w_out=w_out[0], norm2=norm2[0], w_up=w_up[0], conv_w=conv_w[0],
             conv_b=conv_b[0], w_down=w_down[0])
    yp, ys, sp, ss = _layer(x_prompt, x_sample, c_prompt, c_sample, state_gla[0], state_s5_re[0],
                            state_s5_im[0], state_conv[0], w, final_norm)
    stack = lambda t: t[None]
    return (yp, ys, stack(sp[0]), stack(sp[1]), stack(sp[2]), stack(sp[3]),
            stack(ss[0]), stack(ss[1]), stack(ss[2]), stack(ss[3]))
```

```python
import functools
import math

import jax
import jax.numpy as jnp
from jax import lax
from jax.experimental import pallas as pl
from jax.experimental.pallas import tpu as pltpu

F32 = jnp.float32
BF16 = jnp.bfloat16

NORM_EPS = 1e-6
GLA_TAU = 16.0
GLA_CHUNK = 128
GLA_SUB = 4
LOG2_E = 1.4426950408889634
GLA_SEQ_BLOCK = 512
LANES = 128
S5_CHUNK = 16
S5_CHUNK_SAMPLE = 8
VMEM_LIMIT_BYTES = 56 * 1024 * 1024
TM_PROMPT = 512
TM_MIX = 512
TM_FFN = 512
FFN_TF = 512
FFN_TN = 512


def _cparams(*sem):
    return pltpu.CompilerParams(dimension_semantics=sem, vmem_limit_bytes=VMEM_LIMIT_BYTES)


def _dot(a, b):
    return jnp.dot(a.astype(BF16), b.astype(BF16), preferred_element_type=F32)


def _dot_nt(a, b):
    return lax.dot_general(a.astype(BF16), b.astype(BF16), (((1,), (1,)), ((), ())),
                           preferred_element_type=F32)


def _dot_f32(a, b):
    return jnp.dot(a, b, preferred_element_type=F32, precision=lax.Precision.HIGHEST)


def _spread_exact(xs, sel):
    rows = xs[0].shape[0]
    parts = []
    for x in xs:
        hi = x.astype(BF16)
        r1 = x - hi.astype(F32)
        mid = r1.astype(BF16)
        parts += [hi, mid, (r1 - mid.astype(F32)).astype(BF16)]
    y = jnp.dot(jnp.concatenate(parts, axis=0), sel.astype(BF16), preferred_element_type=F32)
    return [y[3 * i * rows:(3 * i + 1) * rows] + y[(3 * i + 1) * rows:(3 * i + 2) * rows]
            + y[(3 * i + 2) * rows:(3 * i + 3) * rows] for i in range(len(xs))]


def _rms(x):
    return x * lax.rsqrt(jnp.mean(x * x, axis=-1, keepdims=True) + NORM_EPS)


def _gelu(x):
    return 0.5 * x * (1.0 + jnp.tanh(math.sqrt(2.0 / math.pi) * (x + 0.044715 * (x * x * x))))


def _sigmoid(x):
    return 1.0 / (1.0 + jnp.exp(-x))


def _ada_kernel(c_ref, w_ref, b_ref, o_ref):
    c = c_ref[...]
    o_ref[...] = _dot(c * _sigmoid(c), w_ref[...]) + b_ref[...]


def _ada(c_all, w_ada, b_ada, tn=1024):
    m, d = c_all.shape
    n = w_ada.shape[1]
    return pl.pallas_call(
        _ada_kernel,
        grid=(n // tn,),
        in_specs=[pl.BlockSpec((m, d), lambda j: (0, 0)),
                  pl.BlockSpec((d, tn), lambda j: (0, j)),
                  pl.BlockSpec((1, tn), lambda j: (0, j))],
        out_specs=pl.BlockSpec((m, tn), lambda j: (0, j)),
        out_shape=jax.ShapeDtypeStruct((m, n), F32),
        compiler_params=_cparams("arbitrary"),
        name="ada",
    )(c_all, w_ada, b_ada)


def _resident(shape, index_map):
    return pl.BlockSpec(shape, index_map, pipeline_mode=pl.Buffered(1))


def _inproj_kernel(x_ref, sh_ref, sc_ref, n1_ref, wm_ref, wu_ref, wa_ref, wa2_ref, ba2_ref,
                   qkvg_ref, u_ref, la_ref, h_scr, *, n_slabs, tn):
    rows = x_ref.shape[0] // n_slabs
    for l in range(n_slabs):
        sl = slice(l * rows, (l + 1) * rows)
        h = _rms(x_ref[sl, :]) * n1_ref[...] * (1.0 + sc_ref[...]) + sh_ref[...]
        h_scr[sl, :] = h.astype(BF16)
    a_lr = jnp.dot(h_scr[...], wa_ref[...], preferred_element_type=F32)
    z = _dot(a_lr, wa2_ref[...]) + ba2_ref[...]
    la_ref[...] = (jnp.minimum(z, 0.0) - jnp.log(1.0 + jnp.exp(-jnp.abs(z)))) / GLA_TAU
    for out_ref, w_ref in ((qkvg_ref, wm_ref), (u_ref, wu_ref)):
        for c0 in range(0, out_ref.shape[1], tn):
            part = jnp.dot(h_scr[...], w_ref[:, c0:c0 + tn], preferred_element_type=F32)
            out_ref[:, c0:c0 + tn] = part.astype(out_ref.dtype)


def _inproj(x, mod, mod_row, norm1, w_in_bf, w_u_bf, w_a_bf, w_a2, b_a2, *, tm, n_slabs, main_w, main_dtype,
            tn=512):
    r, d = x.shape
    sw = w_u_bf.shape[1]
    kw = w_a2.shape[1]
    rank = w_a_bf.shape[1]
    if mod.ndim == 3:
        mspec = lambda k: pl.BlockSpec((None, 1, d), lambda i: (mod_row(i), 0, k))
    else:
        mspec = lambda k: pl.BlockSpec((mod.shape[0], d), lambda i: (0, k))
    kern = functools.partial(_inproj_kernel, n_slabs=n_slabs, tn=tn)
    return pl.pallas_call(
        kern,
        grid=(r // tm,),
        in_specs=[pl.BlockSpec((tm, d), lambda i: (i, 0)),
                  mspec(0), mspec(1),
                  pl.BlockSpec((1, d), lambda i: (0, 0)),
                  _resident((d, main_w), lambda i: (0, 0)),
                  _resident((d, sw), lambda i: (0, 0)),
                  _resident((d, rank), lambda i: (0, 0)),
                  pl.BlockSpec((rank, kw), lambda i: (0, 0)),
                  pl.BlockSpec((1, kw), lambda i: (0, 0))],
        out_specs=[pl.BlockSpec((tm, main_w), lambda i: (i, 0)),
                   pl.BlockSpec((tm, sw), lambda i: (i, 0)),
                   pl.BlockSpec((tm, kw), lambda i: (i, 0))],
        out_shape=[jax.ShapeDtypeStruct((r, main_w), main_dtype),
                   jax.ShapeDtypeStruct((r, sw), F32),
                   jax.ShapeDtypeStruct((r, kw), F32)],
        scratch_shapes=[pltpu.VMEM((tm, d), BF16)],
        compiler_params=_cparams("arbitrary"),
        name="inproj",
    )(x, mod, mod, norm1, w_in_bf, w_u_bf, w_a_bf, w_a2, b_a2)


def _gla_chunk(q, k, v, ga, states, *, h, sub):
    c = q.shape[0]
    dk, dv = q.shape[1] // h, v.shape[1] // h
    rows = lax.broadcasted_iota(jnp.int32, (c, c), 0)
    cols = lax.broadcasted_iota(jnp.int32, (c, c), 1)
    sums, widths = [rows >= cols], []
    w = sub
    while w < c:
        sums.append(cols <= (rows // (2 * w)) * (2 * w) + (w - 1))
        widths.append(w)
        w *= 2
    sums.append(cols >= 0)
    pmat = jnp.concatenate(sums, axis=0).astype(F32).astype(BF16)
    ga_hi = ga.astype(BF16)
    ga_lo = (ga - ga_hi.astype(F32)).astype(BF16)
    tot = (jnp.dot(pmat, ga_hi, preferred_element_type=F32)
           + jnp.dot(pmat, ga_lo, preferred_element_type=F32))
    tot = tot * LOG2_E
    cum = tot[:c]
    last = tot[(len(sums) - 1) * c:]

    row_w = lax.broadcasted_iota(jnp.int32, q.shape, 0)
    lane_o = lax.broadcasted_iota(jnp.int32, (c, LANES), 1)
    by_off = [jnp.zeros((c, LANES), F32) for _ in range(h)]
    for off in range(sub):
        k_sh = k if off == 0 else pltpu.roll(k, off, axis=0)
        c_sh = cum if off == 0 else pltpu.roll(cum, off, axis=0)
        prod = q * k_sh * jnp.exp2(jnp.minimum(cum - c_sh, 0.0))
        for hh in range(h):
            col = jnp.sum(prod[:, hh * dk:(hh + 1) * dk], axis=-1, keepdims=True)
            by_off[hh] = jnp.where(lane_o == c - 1 - off, col, by_off[hh])
    same_sub = rows // sub == cols // sub
    att = []
    for hh in range(h):
        moved = pltpu.roll(by_off[hh], LANES - (c - 1), axis=1, stride=1, stride_axis=0)
        att.append(jnp.where(same_sub, moved[:, :c], 0.0))
    for lvl, w in enumerate(widths):
        ref = tot[(lvl + 1) * c:(lvl + 2) * c]
        odd = (row_w // w) % 2 == 1
        x = jnp.where(odd, q, k) * jnp.exp2(jnp.where(odd, cum - ref, ref - cum))
        pair = (rows // (2 * w) == cols // (2 * w)) & ((rows // w) % 2 == 1) & ((cols // w) % 2 == 0)
        for hh in range(h):
            xh = x[:, hh * dk:(hh + 1) * dk]
            att[hh] = att[hh] + jnp.where(pair, _dot_nt(xh, xh), 0.0)

    q_in = q * jnp.exp2(cum)
    k_out = k * jnp.exp2(last - cum)
    e_last = jnp.exp2(last[0:1])
    pad_rows = -(-(c + 1) // LANES) * LANES
    tail_row = lax.broadcasted_iota(jnp.int32, (pad_rows - c, dk), 0)
    outs, new_states = [], []
    for hh in range(h):
        ck, cv = slice(hh * dk, (hh + 1) * dk), slice(hh * dv, (hh + 1) * dv)
        outs.append(_dot(att[hh], v[:, cv]) + _dot(q_in[:, ck], states[hh]))
        tail = jnp.where(tail_row == 0, e_last[:, ck], 0.0)
        kt = jnp.concatenate([k_out[:, ck], tail], axis=0).T
        new_states.append(states[hh] * kt[:, c:c + 1] + _dot(kt[:, :c], v[:, cv]))
    return outs, new_states


def _gla_finish(o, g, gn):
    return _rms(o) * gn * (g * _sigmoid(g))


def _gla_prompt_kernel(q_ref, k_ref, v_ref, g_ref, la_ref, gn_ref, o_ref, s_ref, s_scr, *, h, dk, dv, chunk, sub,
                       scale):
    blk = pl.program_id(1)

    @pl.when(blk == 0)
    def _():
        s_scr[...] = jnp.zeros_like(s_scr)

    def body(n, carry):
        r = pl.ds(pl.multiple_of(n * chunk, chunk), chunk)
        outs, new_states = _gla_chunk(q_ref[r, :].astype(F32) * scale, k_ref[r, :].astype(F32),
                                      v_ref[r, :].astype(F32), la_ref[r, :],
                                      [s_scr[hh] for hh in range(h)], h=h, sub=sub)
        for hh in range(h):
            cv = slice(hh * dv, (hh + 1) * dv)
            s_scr[hh] = new_states[hh]
            o_ref[r, cv] = _gla_finish(outs[hh], g_ref[r, cv].astype(F32), gn_ref[...]).astype(o_ref.dtype)
        return carry

    lax.fori_loop(0, q_ref.shape[0] // chunk, body, 0)

    @pl.when(blk == pl.num_programs(1) - 1)
    def _():
        s_ref[...] = s_scr[...]


def _gla_prompt(qkvg, la, gla_norm, *, b, l, h, dk, dv):
    kw, gw = h * dk, h * dv
    chunk = math.gcd(l, GLA_CHUNK)
    sub = math.gcd(chunk, GLA_SUB)
    lb = math.gcd(l, GLA_SEQ_BLOCK)
    nlb = l // lb
    kern = functools.partial(_gla_prompt_kernel, h=h, dk=dk, dv=dv, chunk=chunk, sub=sub, scale=dk ** -0.5)
    return pl.pallas_call(
        kern,
        grid=(b, nlb),
        in_specs=[pl.BlockSpec((lb, kw), lambda i, j: (i * nlb + j, 0)),
                  pl.BlockSpec((lb, kw), lambda i, j: (i * nlb + j, 1)),
                  pl.BlockSpec((lb, gw), lambda i, j: (i * nlb + j, 2 * kw // gw)),
                  pl.BlockSpec((lb, gw), lambda i, j: (i * nlb + j, 2 * kw // gw + 1)),
                  pl.BlockSpec((lb, kw), lambda i, j: (i * nlb + j, 0)),
                  pl.BlockSpec((1, dv), lambda i, j: (0, 0))],
        out_specs=[pl.BlockSpec((lb, gw), lambda i, j: (i * nlb + j, 0)),
                   pl.BlockSpec((None, h, dk, dv), lambda i, j: (i, 0, 0, 0))],
        out_shape=[jax.ShapeDtypeStruct((b * l, gw), BF16),
                   jax.ShapeDtypeStruct((b, h, dk, dv), F32)],
        scratch_shapes=[pltpu.VMEM((h, dk, dv), F32)],
        compiler_params=_cparams("arbitrary", "arbitrary"),
        name="gla_prompt",
    )(qkvg, qkvg, qkvg, qkvg, la, gla_norm)


def _gla_sample_kernel(qkvg_ref, la_ref, gn_ref, s0_ref, o_ref, s_ref, *, nb, seq, h, dk, dv, pad, scale):
    bt = s0_ref.shape[0]
    kw, gw = h * dk, h * dv
    in_w = 2 * kw + 2 * gw
    i = pl.program_id(0)

    def rows_of(b, width, col):
        per_row = width // LANES
        return pl.ds(b * per_row + col // LANES, seq, stride=nb * per_row)

    def gather(ref, b, width, col, n_col):
        parts = [ref[rows_of(b, width, col + c), :] for c in range(0, n_col, LANES)]
        x = parts[0] if len(parts) == 1 else jnp.concatenate(parts, axis=1)
        return jnp.concatenate([x, jnp.zeros((pad - seq, n_col), F32)], axis=0)

    def body(bb, carry):
        b = i * bt + bb
        q = gather(qkvg_ref, b, in_w, 0, kw) * scale
        k = gather(qkvg_ref, b, in_w, kw, kw)
        v = gather(qkvg_ref, b, in_w, 2 * kw, gw)
        g = gather(qkvg_ref, b, in_w, 2 * kw + gw, gw)
        ga = gather(la_ref, b, kw, 0, kw)
        outs, new_states = _gla_chunk(q, k, v, ga, [s0_ref[bb, hh] for hh in range(h)], h=h, sub=pad)
        for hh in range(h):
            s_ref[bb, hh] = new_states[hh]
            res = _gla_finish(outs[hh], g[:, hh * dv:(hh + 1) * dv], gn_ref[...])
            for c in range(0, dv, LANES):
                o_ref[rows_of(b, gw, hh * dv + c), :] = res[:seq, c:c + LANES]
        return carry

    lax.fori_loop(0, bt, body, 0, unroll=4)


def _gla_sample(qkvg, la, gla_norm, s0, *, nb, seq, h, dk, dv, bt=8):
    gw = h * dv
    pad = 8
    assert dk % LANES == 0 and dv % LANES == 0
    qkvg, la = qkvg.reshape(-1, LANES), la.reshape(-1, LANES)
    o_rows = nb * seq * gw // LANES
    kern = functools.partial(_gla_sample_kernel, nb=nb, seq=seq, h=h, dk=dk, dv=dv, pad=pad, scale=dk ** -0.5)
    o, s_new = pl.pallas_call(
        kern,
        grid=(nb // bt,),
        in_specs=[pl.BlockSpec(qkvg.shape, lambda i: (0, 0)),
                  pl.BlockSpec(la.shape, lambda i: (0, 0)),
                  pl.BlockSpec((1, dv), lambda i: (0, 0)),
                  pl.BlockSpec((bt, h, dk, dv), lambda i: (i, 0, 0, 0))],
        out_specs=[pl.BlockSpec((o_rows, LANES), lambda i: (0, 0)),
                   pl.BlockSpec((bt, h, dk, dv), lambda i: (i, 0, 0, 0))],
        out_shape=[jax.ShapeDtypeStruct((o_rows, LANES), F32),
                   jax.ShapeDtypeStruct((nb, h, dk, dv), F32)],
        compiler_params=_cparams("arbitrary"),
        name="gla_sample",
    )(qkvg, la, gla_norm, s0)
    return o.reshape(nb * seq, gw), s_new


def _s5_prep_kernel(lamc_ref, lamr_ref, ls_ref, bt_re_ref, bt_im_ref, btile_re_ref, btile_im_ref,
                    ctile_re_ref, ctile_im_ref, d_ref, m_ref, wst_ref, v_ref, lc_ref, *, ch, c_mat, c_real):
    w = ch * c_mat
    p = lamc_ref.shape[1]
    lane_m = lax.broadcasted_iota(jnp.int32, (p, LANES), 1)
    m_f = jnp.where(lane_m <= c_mat, lane_m, 0).astype(F32)
    sel = lax.broadcasted_iota(jnp.int32, (LANES, w), 0)
    tau = lax.broadcasted_iota(jnp.int32, (LANES, w), 1) // ch
    spread_pw = (sel == tau).astype(F32)
    spread_pv = (sel == tau + 1).astype(F32)
    spread_ps = (sel == jnp.maximum(c_real - 1 - tau, 0)).astype(F32)
    rr = lax.broadcasted_iota(jnp.int32, (ch, w), 0)
    cc = lax.broadcasted_iota(jnp.int32, (ch, w), 1)
    spread_ch = (cc % ch == rr).astype(F32)

    def bbar_coef(lam_re, lam_im, lb_re, lb_im):
        den = lam_re * lam_re + lam_im * lam_im
        x, y = lb_re - 1.0, lb_im
        return (x * lam_re + y * lam_im) / den, (y * lam_re - x * lam_im) / den

    gb = m_ref.shape[0]
    dts, tables = [], []
    for gi in range(gb):
        dt = jnp.exp(ls_ref[gi])
        mag = jnp.exp(lamc_ref[gi, :, 0:1] * dt * m_f)
        ang = lamc_ref[gi, :, 1:2] * dt * m_f
        dts.append(dt)
        tables += [mag * jnp.cos(ang), mag * jnp.sin(ang)]
    powers = _spread_exact(tables, jnp.concatenate([spread_pw, spread_pv, spread_ps], axis=1))
    tiles = _spread_exact([r[gi] for gi in range(gb)
                           for r in (ctile_re_ref, ctile_im_ref, btile_re_ref, btile_im_ref)], spread_ch)

    for gi in range(gb):
        dt = dts[gi]
        lam_re, lam_im = lamc_ref[gi, :, 0:1], lamc_ref[gi, :, 1:2]
        t_re, t_im = tables[2 * gi], tables[2 * gi + 1]
        cf_re, cf_im = bbar_coef(lam_re, lam_im, t_re[:, 1:2], t_im[:, 1:2])

        ct_re, ct_im, bt_re, bt_im = tiles[4 * gi:4 * gi + 4]
        pw_re, pw_im = powers[2 * gi][:, 0:w], powers[2 * gi + 1][:, 0:w]
        cl_re = ct_re * pw_re - ct_im * pw_im
        cl_im = ct_re * pw_im + ct_im * pw_re
        pv_re, pv_im = powers[2 * gi][:, w:2 * w], powers[2 * gi + 1][:, w:2 * w]
        v_ref[gi, 0:p, :] = (ct_re * pv_re - ct_im * pv_im).astype(v_ref.dtype)
        v_ref[gi, p:2 * p, :] = (-(ct_re * pv_im + ct_im * pv_re)).astype(v_ref.dtype)

        bb_re = cf_re * bt_re - cf_im * bt_im
        bb_im = cf_re * bt_im + cf_im * bt_re
        ps_re, ps_im = powers[2 * gi][:, 2 * w:], powers[2 * gi + 1][:, 2 * w:]
        wst_ref[gi, 0:p, :] = (bb_re * ps_re - bb_im * ps_im).astype(wst_ref.dtype)
        wst_ref[gi, p:2 * p, :] = (bb_re * ps_im + bb_im * ps_re).astype(wst_ref.dtype)

        lr_re, lr_im = lamr_ref[gi, 0:1, :], lamr_ref[gi, 1:2, :]
        ar, tr = lr_re * dt, lr_im * dt
        ea = jnp.exp(ar)
        rf_re, rf_im = bbar_coef(lr_re, lr_im, ea * jnp.cos(tr), ea * jnp.sin(tr))
        bbt_re = rf_re * bt_re_ref[gi] - rf_im * bt_im_ref[gi]
        bbt_im = rf_re * bt_im_ref[gi] + rf_im * bt_re_ref[gi]
        kcat = _dot_f32(bbt_re, cl_re) - _dot_f32(bbt_im, cl_im)
        kcat = kcat + jnp.where(rr == cc, d_ref[gi], 0.0)
        for i in range(c_mat):
            shifted = kcat if i == 0 else pltpu.roll(kcat, ch * i, axis=1)
            m_ref[gi, i * ch:(i + 1) * ch, :] = jnp.where(cc >= ch * i, shifted, 0.0).astype(m_ref.dtype)

        ec = jnp.exp(ar * float(c_real))
        lc_ref[gi, 0:1, :] = ec * jnp.cos(tr * float(c_real))
        lc_ref[gi, 1:2, :] = ec * jnp.sin(tr * float(c_real))


def _s5_prep(lam_re, lam_im, log_step, b_re, b_im, c_re, c_im, d_skip, *, c_mat, c_real):
    g, p, ch = b_re.shape
    w = ch * c_mat
    gb = LANES // ch
    lam_col = jnp.stack([lam_re, lam_im], axis=-1)
    lam_row = jnp.stack([lam_re, lam_im], axis=1)
    ls = log_step.reshape(g, 1, 1)
    bt = lambda t: jnp.swapaxes(t, 1, 2)
    d_row = jnp.pad(d_skip.reshape(g, 1, ch), ((0, 0), (0, 0), (0, w - ch)))
    spec3 = lambda s: pl.BlockSpec((gb,) + s, lambda i: (i, 0, 0))
    kern = functools.partial(_s5_prep_kernel, ch=ch, c_mat=c_mat, c_real=c_real)
    return pl.pallas_call(
        kern,
        grid=(g // gb,),
        in_specs=[spec3((p, 2)), spec3((2, p)), spec3((1, 1)), spec3((ch, p)), spec3((ch, p)),
                  spec3((p, ch)), spec3((p, ch)), spec3((p, ch)), spec3((p, ch)), spec3((1, w))],
        out_specs=[spec3((w, w)), spec3((2 * p, w)), spec3((2 * p, w)), spec3((2, p))],
        out_shape=[jax.ShapeDtypeStruct((g, w, w), BF16), jax.ShapeDtypeStruct((g, 2 * p, w), BF16),
                   jax.ShapeDtypeStruct((g, 2 * p, w), BF16), jax.ShapeDtypeStruct((g, 2, p), F32)],
        compiler_params=_cparams("arbitrary"),
        name="s5_prep",
    )(lam_col, lam_row, ls, bt(b_re), bt(b_im), b_re, b_im, bt(c_re), bt(c_im), d_row)


def _cmul_rows(z, lr, li):
    p = lr.shape[1]
    coef_a = jnp.concatenate([lr, lr], axis=1)
    coef_b = jnp.concatenate([-li, li], axis=1)
    return z * coef_a + pltpu.roll(z, p, axis=1) * coef_b


def _regroup(parts, gi, ch):
    return jnp.concatenate([t[:, gi * ch:(gi + 1) * ch] for t in parts], axis=1)


def _s5_prompt_kernel(u_ref, m_ref, wst_ref, v_ref, lc_ref, y_ref, hre_ref, him_ref, *, ch, c, n_seq, n_chunk):
    gb = m_ref.shape[0]
    r = n_seq * n_chunk
    p = lc_ref.shape[2]
    toks = [pltpu.bitcast(u_ref[pl.ds(i, r, stride=c), :].astype(BF16), jnp.uint32) for i in range(c)]
    pos = lax.broadcasted_iota(jnp.int32, (r, 2 * p), 0) % n_chunk
    ys, h_re, h_im = [], [], []
    for gi in range(gb):
        x = pltpu.bitcast(_regroup(toks, gi, ch), BF16)
        z = lax.dot_general(x, wst_ref[gi], (((1,), (1,)), ((), ())), preferred_element_type=F32)
        lr, li = lc_ref[gi, 0:1, :], lc_ref[gi, 1:2, :]
        d = 1
        while d < n_chunk:
            zs = jnp.where(pos >= d, pltpu.roll(z, d, axis=0), 0.0)
            z = z + _cmul_rows(zs, lr, li)
            lr, li = lr * lr - li * li, 2.0 * lr * li
            d *= 2
        h_in = jnp.where(pos >= 1, pltpu.roll(z, 1, axis=0), 0.0)
        y = (jnp.dot(x, m_ref[gi], preferred_element_type=F32)
             + jnp.dot(h_in.astype(BF16), v_ref[gi], preferred_element_type=F32))
        ys.append(pltpu.bitcast(y.astype(BF16), jnp.uint32))
        hf = jnp.concatenate([z[(s + 1) * n_chunk - 1:(s + 1) * n_chunk] for s in range(n_seq)], axis=0)
        h_re.append(hf[:, :p])
        h_im.append(hf[:, p:])
    for j in range(c):
        y_ref[pl.ds(j, r, stride=c), :] = pltpu.bitcast(_regroup(ys, j, ch), BF16).astype(F32)
    hre_ref[...] = jnp.concatenate(h_re, axis=1)
    him_ref[...] = jnp.concatenate(h_im, axis=1)


def _s5_prompt(u, m, wst, v, lc, *, n_seq, n_chunk, c, ch):
    t, sw = u.shape
    g, w, _ = m.shape
    p2 = wst.shape[1]
    p = p2 // 2
    gb = LANES // ch
    spec3 = lambda s: pl.BlockSpec((gb,) + s, lambda i: (i, 0, 0))
    kern = functools.partial(_s5_prompt_kernel, ch=ch, c=c, n_seq=n_seq, n_chunk=n_chunk)
    return pl.pallas_call(
        kern,
        grid=(g // gb,),
        in_specs=[pl.BlockSpec((t, LANES), lambda i: (0, i)),
                  spec3((w, w)), spec3((p2, w)), spec3((p2, w)), spec3((2, p))],
        out_specs=[pl.BlockSpec((t, LANES), lambda i: (0, i)),
                   pl.BlockSpec((n_seq, gb * p), lambda i: (0, i)),
                   pl.BlockSpec((n_seq, gb * p), lambda i: (0, i))],
        out_shape=[jax.ShapeDtypeStruct((t, sw), F32), jax.ShapeDtypeStruct((n_seq, g * p), F32),
                   jax.ShapeDtypeStruct((n_seq, g * p), F32)],
        compiler_params=_cparams("arbitrary"),
        name="s5_prompt",
    )(u, m, wst, v, lc)


def _s5_sample_kernel(u_ref, sre_ref, sim_ref, m_ref, wst_ref, v_ref, lc_ref, y_ref, hre_ref, him_ref, *,
                      ch, seq, nb):
    gb = m_ref.shape[0]
    p = lc_ref.shape[2]
    w = m_ref.shape[1]
    toks = [u_ref[l * nb:(l + 1) * nb, :] for l in range(seq)]
    ys, h_re, h_im = [], [], []
    for gi in range(gb):
        x = jnp.concatenate([_regroup(toks, gi, ch), jnp.zeros((nb, w - seq * ch), F32)], axis=1).astype(BF16)
        h0 = jnp.concatenate([sre_ref[:, gi * p:(gi + 1) * p], sim_ref[:, gi * p:(gi + 1) * p]], axis=1)
        ys.append(jnp.dot(x, m_ref[gi], preferred_element_type=F32)
                  + jnp.dot(h0.astype(BF16), v_ref[gi], preferred_element_type=F32))
        hf = (_cmul_rows(h0, lc_ref[gi, 0:1, :], lc_ref[gi, 1:2, :])
              + lax.dot_general(x, wst_ref[gi], (((1,), (1,)), ((), ())), preferred_element_type=F32))
        h_re.append(hf[:, :p])
        h_im.append(hf[:, p:])
    for l in range(seq):
        y_ref[l * nb:(l + 1) * nb, :] = _regroup(ys, l, ch)
    hre_ref[...] = jnp.concatenate(h_re, axis=1)
    him_ref[...] = jnp.concatenate(h_im, axis=1)


def _s5_sample(u, st_re, st_im, m, wst, v, lc, *, seq, nb, ch):
    t, sw = u.shape
    g, w, _ = m.shape
    p2 = wst.shape[1]
    p = p2 // 2
    gb = LANES // ch
    spec3 = lambda s: pl.BlockSpec((gb,) + s, lambda i: (i, 0, 0))
    sspec = pl.BlockSpec((nb, gb * p), lambda i: (0, i))
    kern = functools.partial(_s5_sample_kernel, ch=ch, seq=seq, nb=nb)
    return pl.pallas_call(
        kern,
        grid=(g // gb,),
        in_specs=[pl.BlockSpec((t, LANES), lambda i: (0, i)), sspec, sspec,
                  spec3((w, w)), spec3((p2, w)), spec3((p2, w)), spec3((2, p))],
        out_specs=[pl.BlockSpec((t, LANES), lambda i: (0, i)), sspec, sspec],
        out_shape=[jax.ShapeDtypeStruct((t, sw), F32), jax.ShapeDtypeStruct((nb, g * p), F32),
                   jax.ShapeDtypeStruct((nb, g * p), F32)],
        compiler_params=_cparams("arbitrary"),
        name="s5_sample",
    )(u, st_re, st_im, m, wst, v, lc)


def _mix_kernel(o_ref, y_ref, x_ref, g1_ref, sh2_ref, sc2_ref, n2_ref, wglu_ref, bglu_ref, wo_ref,
                x1_ref, h2_ref, *, n_slabs, tn):
    rows = x_ref.shape[0] // n_slabs
    gw = o_ref.shape[1]
    yg = _gelu(y_ref[...])
    z = (yg * _sigmoid(_dot(yg, wglu_ref[...]) + bglu_ref[...])).astype(BF16)
    ob = o_ref[...].astype(BF16)
    for c0 in range(0, x_ref.shape[1], tn):
        cols = slice(c0, c0 + tn)
        mix = (jnp.dot(ob, wo_ref[0:gw, cols], preferred_element_type=F32)
               + jnp.dot(z, wo_ref[gw:, cols], preferred_element_type=F32))
        for l in range(n_slabs):
            sl = slice(l * rows, (l + 1) * rows)
            x1_ref[sl, cols] = x_ref[sl, cols] + g1_ref[:, cols] * mix[sl, :]
    for l in range(n_slabs):
        sl = slice(l * rows, (l + 1) * rows)
        h2 = _rms(x1_ref[sl, :]) * n2_ref[...] * (1.0 + sc2_ref[...]) + sh2_ref[...]
        h2_ref[sl, :] = h2.astype(BF16)


def _mix(o, y, x, mod, mod_row, norm2, w_glu_bf, b_glu, w_out_bf, *, tm, n_slabs, tn=512):
    r, d = x.shape
    gw = o.shape[1]
    sw = y.shape[1]
    if mod.ndim == 3:
        mspec = lambda k: pl.BlockSpec((None, 1, d), lambda i: (mod_row(i), 0, k))
    else:
        mspec = lambda k: pl.BlockSpec((mod.shape[0], d), lambda i: (0, k))
    kern = functools.partial(_mix_kernel, n_slabs=n_slabs, tn=tn)
    return pl.pallas_call(
        kern,
        grid=(r // tm,),
        in_specs=[pl.BlockSpec((tm, gw), lambda i: (i, 0)),
                  pl.BlockSpec((tm, sw), lambda i: (i, 0)),
                  pl.BlockSpec((tm, d), lambda i: (i, 0)),
                  mspec(2), mspec(3), mspec(4),
                  pl.BlockSpec((1, d), lambda i: (0, 0)),
                  _resident((sw, sw), lambda i: (0, 0)),
                  pl.BlockSpec((1, sw), lambda i: (0, 0)),
                  _resident((gw + sw, d), lambda i: (0, 0))],
        out_specs=[pl.BlockSpec((tm, d), lambda i: (i, 0)),
                   pl.BlockSpec((tm, d), lambda i: (i, 0))],
        out_shape=[jax.ShapeDtypeStruct((r, d), F32), jax.ShapeDtypeStruct((r, d), BF16)],
        compiler_params=_cparams("arbitrary"),
        name="mix",
    )(o, y, x, mod, mod, mod, norm2, w_glu_bf, b_glu, w_out_bf)


def _conv3(up, ext, cw_ref, cb_ref, off1, off2):
    n = up.shape[0]
    return (cw_ref[2:3, :] * up + cw_ref[1:2, :] * ext[off1:off1 + n] + cw_ref[0:1, :] * ext[off2:off2 + n]
            + cb_ref[...])


def _ffn_prompt_kernel(h_ref, x1_ref, g2_ref, fn_ref, wua_ref, wug_ref, cwa_ref, cwg_ref,
                       cba_ref, cbg_ref, wd_ref, y_ref, cs_ref, act_scr, x2_scr, carry_scr, ext_scr, *,
                       tiles_per_seq, nf, nd, tf, tn, sub_tf):
    i, j = pl.program_id(0), pl.program_id(1)
    keep = carry_scr.shape[1]

    @pl.when((i == 0) & (j == 0))
    def _():
        carry_scr[...] = jnp.zeros_like(carry_scr)

    @pl.when(j < nf)
    def _():
        first = (i % tiles_per_seq) == 0
        h = h_ref[...]
        tm = h.shape[0]
        for ci, c0 in enumerate(range(0, tf, sub_tf)):
            cols = slice(c0, c0 + sub_tf)
            ccols = pl.ds(pl.multiple_of(j * tf + c0, sub_tf), sub_tf)

            def half(idx, w_ref, cw_ref, cb_ref):
                ext = ext_scr.at[ci, idx]
                up = jnp.dot(h, w_ref[:, cols], preferred_element_type=F32)
                ext[0:keep, :] = jnp.where(first, 0.0, carry_scr[idx, :, ccols])
                ext[keep:, :] = up
                carry_scr[idx, :, ccols] = up[tm - keep:]
                conv = (cw_ref[2:3, cols] * up + cw_ref[1:2, cols] * ext[keep - 1:keep - 1 + tm, :]
                        + cw_ref[0:1, cols] * ext[keep - 2:keep - 2 + tm, :] + cb_ref[:, cols])
                return conv, up

            a, up_a = half(0, wua_ref, cwa_ref, cba_ref)
            g, up_g = half(1, wug_ref, cwg_ref, cbg_ref)
            for rr in range(2):
                row = tm - 2 + rr
                cs_ref[rr, :, cols] = jnp.concatenate([up_a[row:row + 1], up_g[row:row + 1]], axis=0)
            act_scr[:, ccols] = (_gelu(a) * g).astype(BF16)

    @pl.when(j >= nf)
    def _():
        cols = pl.ds(pl.multiple_of((j - nf) * tn, tn), tn)
        ff = jnp.dot(act_scr[...], wd_ref[...], preferred_element_type=F32)
        x2_scr[:, cols] = x1_ref[...] + g2_ref[:, cols] * ff

    @pl.when(j == nf + nd - 1)
    def _():
        y_ref[...] = _rms(x2_scr[...]) * fn_ref[...]


def _ffn_prompt(h2, x1, mod, final_norm, w_up_bf, conv_w, conv_b, w_down_bf, *, b, l, tm, tf=FFN_TF, tn=FFN_TN):
    r, d = x1.shape
    dff = w_down_bf.shape[0]
    nf, nd = dff // tf, d // tn
    tps = l // tm
    up_j = lambda j: jnp.minimum(j, nf - 1)
    down_j = lambda j: jnp.maximum(j - nf, 0)
    sub_tf = math.gcd(tf, 2 * LANES)
    kern = functools.partial(_ffn_prompt_kernel, tiles_per_seq=tps, nf=nf, nd=nd, tf=tf, tn=tn, sub_tf=sub_tf)
    return pl.pallas_call(
        kern,
        grid=(r // tm, nf + nd),
        in_specs=[pl.BlockSpec((tm, d), lambda i, j: (i, 0)),
                  pl.BlockSpec((tm, tn), lambda i, j: (i, down_j(j))),
                  pl.BlockSpec((None, 1, d), lambda i, j: (i // tps, 0, 5)),
                  pl.BlockSpec((1, d), lambda i, j: (0, 0)),
                  pl.BlockSpec((d, tf), lambda i, j: (0, up_j(j))),
                  pl.BlockSpec((d, tf), lambda i, j: (0, nf + up_j(j))),
                  pl.BlockSpec((3, tf), lambda i, j: (0, up_j(j))),
                  pl.BlockSpec((3, tf), lambda i, j: (0, nf + up_j(j))),
                  pl.BlockSpec((1, tf), lambda i, j: (0, up_j(j))),
                  pl.BlockSpec((1, tf), lambda i, j: (0, nf + up_j(j))),
                  pl.BlockSpec((dff, tn), lambda i, j: (0, down_j(j)))],
        out_specs=[pl.BlockSpec((tm, d), lambda i, j: (i, 0)),
                   pl.BlockSpec((None, 2, 2, tf), lambda i, j: (i, 0, 0, up_j(j)))],
        out_shape=[jax.ShapeDtypeStruct((r, d), F32), jax.ShapeDtypeStruct((r // tm, 2, 2, dff), F32)],
        scratch_shapes=[pltpu.VMEM((tm, dff), BF16), pltpu.VMEM((tm, d), F32), pltpu.VMEM((2, 8, dff), F32),
                        pltpu.VMEM((tf // sub_tf, 2, 8 + tm, sub_tf), F32)],
        compiler_params=_cparams("arbitrary", "arbitrary"),
        name="ffn_prompt",
    )(h2, x1, mod, final_norm, w_up_bf, w_up_bf, conv_w, conv_w, conv_b, conv_b, w_down_bf)


def _ffn_sample_kernel(h_ref, x1_ref, g2_ref, fn_ref, st0a_ref, st1a_ref, st0g_ref, st1g_ref, wua_ref, wug_ref,
                       cwa_ref, cwg_ref, cba_ref, cbg_ref, wd_ref, y_ref, cs0a_ref, cs1a_ref, cs0g_ref, cs1g_ref,
                       act_scr, x2_scr, *, nb, seq, nf, tf, tn):
    j = pl.program_id(0)

    @pl.when(j < nf)
    def _():
        h = h_ref[...]

        def half(w_ref, st0_ref, st1_ref, cw_ref, cb_ref, cs0_ref, cs1_ref):
            up = jnp.dot(h, w_ref[...], preferred_element_type=F32)
            ext = jnp.concatenate([st0_ref[...], st1_ref[...], up], axis=0)
            cs0_ref[...] = up[(seq - 2) * nb:(seq - 1) * nb]
            cs1_ref[...] = up[(seq - 1) * nb:]
            return _conv3(up, ext, cw_ref, cb_ref, nb, 0)

        a = half(wua_ref, st0a_ref, st1a_ref, cwa_ref, cba_ref, cs0a_ref, cs1a_ref)
        g = half(wug_ref, st0g_ref, st1g_ref, cwg_ref, cbg_ref, cs0g_ref, cs1g_ref)
        act_scr[:, pl.ds(pl.multiple_of(j * tf, tf), tf)] = (_gelu(a) * g).astype(BF16)

    @pl.when(j >= nf)
    def _():
        cols = pl.ds(pl.multiple_of((j - nf) * tn, tn), tn)
        ff = jnp.dot(act_scr[...], wd_ref[...], preferred_element_type=F32)
        for l in range(seq):
            sl = slice(l * nb, (l + 1) * nb)
            x2_scr[sl, cols] = x1_ref[sl, :] + g2_ref[:, cols] * ff[sl, :]

    @pl.when(j == pl.num_programs(0) - 1)
    def _():
        y_ref[...] = _rms(x2_scr[...]) * fn_ref[...]


def _ffn_sample(h2, x1, mod, final_norm, st, w_up_bf, conv_w, conv_b, w_down_bf, *, nb, seq, tf=FFN_TF, tn=FFN_TN):
    r, d = x1.shape
    dff = w_down_bf.shape[0]
    nf, nd = dff // tf, d // tn
    up_j = lambda j: jnp.minimum(j, nf - 1)
    down_j = lambda j: jnp.maximum(j - nf, 0)
    st_spec = lambda row, half: pl.BlockSpec((nb, tf), lambda j: (0, (2 * row + half) * nf + up_j(j)))
    cs_spec = pl.BlockSpec((nb, tf), lambda j: (0, up_j(j)))
    cs_shape = jax.ShapeDtypeStruct((nb, dff), F32)
    kern = functools.partial(_ffn_sample_kernel, nb=nb, seq=seq, nf=nf, tf=tf, tn=tn)
    return pl.pallas_call(
        kern,
        grid=(nf + nd,),
        in_specs=[pl.BlockSpec((r, d), lambda j: (0, 0)),
                  pl.BlockSpec((r, tn), lambda j: (0, down_j(j))),
                  pl.BlockSpec((nb, d), lambda j: (0, 5)),
                  pl.BlockSpec((1, d), lambda j: (0, 0)),
                  st_spec(0, 0), st_spec(1, 0), st_spec(0, 1), st_spec(1, 1),
                  pl.BlockSpec((d, tf), lambda j: (0, up_j(j))),
                  pl.BlockSpec((d, tf), lambda j: (0, nf + up_j(j))),
                  pl.BlockSpec((3, tf), lambda j: (0, up_j(j))),
                  pl.BlockSpec((3, tf), lambda j: (0, nf + up_j(j))),
                  pl.BlockSpec((1, tf), lambda j: (0, up_j(j))),
                  pl.BlockSpec((1, tf), lambda j: (0, nf + up_j(j))),
                  pl.BlockSpec((dff, tn), lambda j: (0, down_j(j)))],
        out_specs=[pl.BlockSpec((r, d), lambda j: (0, 0)), cs_spec, cs_spec, cs_spec, cs_spec],
        out_shape=[jax.ShapeDtypeStruct((r, d), F32), cs_shape, cs_shape, cs_shape, cs_shape],
        scratch_shapes=[pltpu.VMEM((r, dff), BF16), pltpu.VMEM((r, d), F32)],
        compiler_params=_cparams("arbitrary"),
        name="ffn_sample",
    )(h2, x1, mod, final_norm, st, st, st, st, w_up_bf, w_up_bf, conv_w, conv_w, conv_b, conv_b, w_down_bf)


def _layer(xp, xs, cp, cs, st_gla, st_re, st_im, st_conv, w, final_norm):
    b, l, d = xp.shape
    nb, seq, _ = xs.shape
    _, h, dk, dv = st_gla.shape
    kw, gw = h * dk, h * dv
    g, p, ch = w['s5_b_re'].shape
    sw = g * ch
    rank = w['w_a2'].shape[0]
    dff = w['w_down'].shape[0]
    main_w = 2 * kw + 2 * gw
    row = lambda t: t.reshape(1, -1)

    w_in_bf = w['w_in'].astype(BF16)
    w_u_bf = w['w_in'][:, main_w + rank:].astype(BF16)
    w_a_bf = w['w_in'][:, main_w:main_w + rank].astype(BF16)
    w_glu_bf = w['w_glu'].astype(BF16)
    w_out_bf = w['w_out'].astype(BF16)
    w_up_bf = w['w_up'].astype(BF16)
    w_down_bf = w['w_down'].astype(BF16)

    n_c = b + nb
    n_c_pad = -(-n_c // 8) * 8
    c_all = jnp.concatenate([cp, cs, jnp.zeros((n_c_pad - n_c, d), F32)], axis=0)
    mod = _ada(c_all, w['w_ada'], row(w['b_ada']))
    mod_p = mod[:b].reshape(b, 1, 6 * d)
    mod_s = mod[b:n_c]

    xp2 = xp.reshape(b * l, d)
    xs2 = jnp.swapaxes(xs, 0, 1).reshape(seq * nb, d)
    tm_p = min(TM_PROMPT, l)
    tps = l // tm_p
    seq_of = lambda i: i // tps

    proj = functools.partial(_inproj, norm1=row(w['norm1']), w_in_bf=w_in_bf, w_u_bf=w_u_bf, w_a_bf=w_a_bf,
                             w_a2=w['w_a2'], b_a2=row(w['b_a2']), main_w=main_w)
    qkvg_p, u_p, la_p = proj(xp2, mod_p, seq_of, tm=tm_p, n_slabs=1, main_dtype=BF16)
    qkvg_s, u_s, la_s = proj(xs2, mod_s, None, tm=seq * nb, n_slabs=seq, main_dtype=F32)

    gn = row(w['gla_norm'])
    o_p, gla_p = _gla_prompt(qkvg_p, la_p, gn, b=b, l=l, h=h, dk=dk, dv=dv)
    o_s, gla_s = _gla_sample(qkvg_s, la_s, gn, st_gla, nb=nb, seq=seq, h=h, dk=dk, dv=dv)

    s5w = (w['s5_lam_re'], w['s5_lam_im'], w['s5_log_step'], w['s5_b_re'], w['s5_b_im'],
           w['s5_c_re'], w['s5_c_im'], w['s5_d'])
    cp_ = math.gcd(l, S5_CHUNK)
    n_chunk = l // cp_
    m_p, wst_p, v_p, lc_p = _s5_prep(*s5w, c_mat=cp_, c_real=cp_)
    y5_p, re_p, im_p = _s5_prompt(u_p, m_p, wst_p, v_p, lc_p, n_seq=b, n_chunk=n_chunk, c=cp_, ch=ch)
    re_p, im_p = re_p.reshape(b, g, p), im_p.reshape(b, g, p)

    m_s, wst_s, v_s, lc_s = _s5_prep(*s5w, c_mat=S5_CHUNK_SAMPLE, c_real=seq)
    y5_s, re_s, im_s = _s5_sample(u_s, st_re.reshape(nb, g * p), st_im.reshape(nb, g * p), m_s, wst_s, v_s, lc_s,
                                  seq=seq, nb=nb, ch=ch)
    re_s, im_s = re_s.reshape(nb, g, p), im_s.reshape(nb, g, p)

    mixer = functools.partial(_mix, norm2=row(w['norm2']), w_glu_bf=w_glu_bf, b_glu=row(w['b_glu']),
                              w_out_bf=w_out_bf)
    tm_mix = min(TM_MIX, l)
    x1_p, h2_p = mixer(o_p, y5_p, xp2, mod_p, lambda i: i // (l // tm_mix), tm=tm_mix, n_slabs=1)
    x1_s, h2_s = mixer(o_s, y5_s, xs2, mod_s, None, tm=seq * nb, n_slabs=seq)

    fn = row(final_norm)
    conv_w, conv_b = w['conv_w'], row(w['conv_b'])
    tm_f = min(TM_FFN, l)
    yp, cs_p = _ffn_prompt(h2_p, x1_p, mod_p, fn, w_up_bf, conv_w, conv_b, w_down_bf, b=b, l=l, tm=tm_f)
    conv_p = cs_p[l // tm_f - 1::l // tm_f].reshape(b, 2, 2 * dff)
    ys, cs0a, cs1a, cs0g, cs1g = _ffn_sample(h2_s, x1_s, mod_s, fn, st_conv.reshape(nb, 4 * dff), w_up_bf,
                                             conv_w, conv_b, w_down_bf, nb=nb, seq=seq)
    conv_s = jnp.concatenate([cs0a, cs0g, cs1a, cs1g], axis=1).reshape(nb, 2, 2 * dff)

    yp = yp.reshape(b, l, d)
    ys = jnp.swapaxes(ys.reshape(seq, nb, d), 0, 1)
    return yp, ys, (gla_p, re_p, im_p, conv_p), (gla_s, re_s, im_s, conv_s)


def kernel(x_prompt, x_sample, c_prompt, c_sample, state_gla, state_s5_re, state_s5_im, state_conv, w_ada, b_ada, norm1, w_in, w_a2, b_a2, gla_norm, s5_lam_re, s5_lam_im, s5_log_step, s5_b_re, s5_b_im, s5_c_re, s5_c_im, s5_d, w_glu, b_glu, w_out, norm2, w_up, conv_w, conv_b, w_down, final_norm):
    depth = w_ada.shape[0]
    assert depth == 1, "the final norm is fused into the last layer's FFN; only depth 1 is wired up"
    w = dict(w_ada=w_ada[0], b_ada=b_ada[0], norm1=norm1[0], w_in=w_in[0], w_a2=w_a2[0], b_a2=b_a2[0],
             gla_norm=gla_norm[0], s5_lam_re=s5_lam_re[0], s5_lam_im=s5_lam_im[0], s5_log_step=s5_log_step[0],
             s5_b_re=s5_b_re[0], s5_b_im=s5_b_im[0], s5_c_re=s5_c_re[0], s5_c_im=s5_c_im[0], s5_d=s5_d[0],
             w_glu=w_glu[0], b_glu=b_glu[0], w_out=w_out[0], norm2=norm2[0], w_up=w_up[0], conv_w=conv_w[0],
             conv_b=conv_b[0], w_down=w_down[0])
    yp, ys, sp, ss = _layer(x_prompt, x_sample, c_prompt, c_sample, state_gla[0], state_s5_re[0],
                            state_s5_im[0], state_conv[0], w, final_norm)
    stack = lambda t: t[None]
    return (yp, ys, stack(sp[0]), stack(sp[1]), stack(sp[2]), stack(sp[3]),
            stack(ss[0]), stack(ss[1]), stack(ss[2]), stack(ss[3]))
```
